```python
import math
import jax, jax.numpy as jnp
from jax import lax
import numpy as np

D_MODEL = 1024
BATCH = 1
SEQ = 16384
DEPTH = 2

CHUNK = 64
N_A_LAYERS = DEPTH // 2
N_B_LAYERS = DEPTH - N_A_LAYERS
SSM_GROUP = 16
N_GROUPS = D_MODEL // SSM_GROUP
SSM_STATE = 64
DT_MIN = 1e-3
DT_MAX = 1e-1
N_HEADS = 8
HEAD_DIM = 64
V_DIM = 2 * HEAD_DIM
ROT_DIM = HEAD_DIM // 4
ROPE_THETA = 500000.0
Q_BLOCK = 128
LAMBDA_INIT_STD = 0.1
FFN_HIDDEN = ((8 * D_MODEL + 3 * 256 - 1) // (3 * 256)) * 256
EPS = 1e-6

kernel_name = "s5_then_shared_kv_diff_attention_trunk"


def rmsnorm(x, g):
    xf = x.astype(jnp.float32)
    xf = xf * lax.rsqrt(jnp.mean(xf * xf, axis=-1, keepdims=True) + EPS)
    return (xf * g.astype(jnp.float32)).astype(x.dtype)


def apply_partial_rope(x, pos):
    half = ROT_DIM // 2
    inv_freq = ROPE_THETA ** (-jnp.arange(half, dtype=jnp.float32) * 2.0 / ROT_DIM)
    ang = pos.astype(jnp.float32)[:, None] * inv_freq[None, :]
    cos = jnp.cos(ang)[None, :, None, :]
    sin = jnp.sin(ang)[None, :, None, :]
    xf = x.astype(jnp.float32)
    x1 = xf[..., :half]
    x2 = xf[..., half:ROT_DIM]
    rest = xf[..., ROT_DIM:]
    return jnp.concatenate([x1 * cos - x2 * sin, x2 * cos + x1 * sin, rest], axis=-1)


def swiglu(hn, w1, w3, w2):
    return (jax.nn.silu(hn @ w1) * (hn @ w3)) @ w2


def _cdiag_combine(e1, e2):
    a1r, a1i, b1r, b1i = e1
    a2r, a2i, b2r, b2i = e2
    return (a2r * a1r - a2i * a1i,
            a2r * a1i + a2i * a1r,
            a2r * b1r - a2i * b1i + b2r,
            a2r * b1i + a2i * b1r + b2i)


def s5_mixer(u, lam_re, lam_im, log_dt, b_re, b_im, c_re, c_im, d_skip, w_glu):
    f32 = jnp.float32
    bsz, L, _ = u.shape
    uf = u.astype(f32)
    lr, li = lam_re.astype(f32), lam_im.astype(f32)
    dt = jnp.exp(log_dt.astype(f32))[:, None]
    mag = jnp.exp(lr * dt)
    abar_re, abar_im = mag * jnp.cos(li * dt), mag * jnp.sin(li * dt)
    nr, ni = abar_re - 1.0, abar_im
    den = lr * lr + li * li
    fr = (nr * lr + ni * li) / den
    fi = (ni * lr - nr * li) / den
    br, bi = b_re.astype(f32), b_im.astype(f32)
    bb_re = fr[..., None] * br - fi[..., None] * bi
    bb_im = fr[..., None] * bi + fi[..., None] * br
    ug = uf.reshape(bsz, L, N_GROUPS, SSM_GROUP)
    bu_re = jnp.einsum('blgc,gpc->blgp', ug, bb_re)
    bu_im = jnp.einsum('blgc,gpc->blgp', ug, bb_im)
    a_re = jnp.broadcast_to(abar_re, bu_re.shape)
    a_im = jnp.broadcast_to(abar_im, bu_im.shape)
    _, _, s_re, s_im = lax.associative_scan(_cdiag_combine, (a_re, a_im, bu_re, bu_im), axis=1)
    y = (jnp.einsum('blgp,gcp->blgc', s_re, c_re.astype(f32))
         - jnp.einsum('blgp,gcp->blgc', s_im, c_im.astype(f32)))
    y = y.reshape(bsz, L, D_MODEL) + d_skip.astype(f32) * uf
    z = jax.nn.gelu(y)
    za, zb = jnp.split(z @ w_glu.astype(f32), 2, axis=-1)
    return (za * jax.nn.sigmoid(zb)).astype(u.dtype)


def shared_kv(h, kv_norm_g, w_kv, pos):
    bsz, L, _ = h.shape
    kv = rmsnorm(h, kv_norm_g) @ w_kv
    hk = N_HEADS * HEAD_DIM
    k1 = kv[..., :hk].reshape(bsz, L, N_HEADS, HEAD_DIM)
    k2 = kv[..., hk:2 * hk].reshape(bsz, L, N_HEADS, HEAD_DIM)
    v = kv[..., 2 * hk:].reshape(bsz, L, N_HEADS, V_DIM).astype(jnp.float32)
    return apply_partial_rope(k1, pos), apply_partial_rope(k2, pos), v


def diff_attention(hn, k1, k2, v, w_q, lq1, lk1, lq2, lk2, subln_g, w_o, lam_init, pos):
    f32 = jnp.float32
    bsz, L, _ = hn.shape
    hq = N_HEADS * HEAD_DIM
    q = hn @ w_q
    q1 = apply_partial_rope(q[..., :hq].reshape(bsz, L, N_HEADS, HEAD_DIM), pos)
    q2 = apply_partial_rope(q[..., hq:].reshape(bsz, L, N_HEADS, HEAD_DIM), pos)
    lam = (jnp.exp(jnp.sum(lq1.astype(f32) * lk1.astype(f32)))
           - jnp.exp(jnp.sum(lq2.astype(f32) * lk2.astype(f32))) + lam_init)
    nblk = L // Q_BLOCK
    q1b = q1.reshape(bsz, nblk, Q_BLOCK, N_HEADS, HEAD_DIM).transpose(1, 0, 2, 3, 4)
    q2b = q2.reshape(bsz, nblk, Q_BLOCK, N_HEADS, HEAD_DIM).transpose(1, 0, 2, 3, 4)
    qpos = pos.reshape(nblk, Q_BLOCK)
    k_chunk = pos // CHUNK
    scale = HEAD_DIM ** -0.5

    def block(args):
        q1c, q2c, qp = args
        mask = (k_chunk[None, :] <= (qp // CHUNK)[:, None])[None, None]
        s1 = jnp.einsum('bqhd,bkhd->bhqk', q1c, k1) * scale
        s2 = jnp.einsum('bqhd,bkhd->bhqk', q2c, k2) * scale
        p1 = jax.nn.softmax(jnp.where(mask, s1, -jnp.inf), axis=-1)
        p2 = jax.nn.softmax(jnp.where(mask, s2, -jnp.inf), axis=-1)
        return jnp.einsum('bhqk,bkhe->bqhe', p1 - lam * p2, v)

    o = lax.map(block, (q1b, q2b, qpos))
    o = o.transpose(1, 0, 2, 3, 4).reshape(bsz, L, N_HEADS, V_DIM)
    o = rmsnorm(o, subln_g) * (1.0 - lam_init)
    return o.reshape(bsz, L, N_HEADS * V_DIM).astype(hn.dtype) @ w_o


def setup_inputs(seed: int = 0) -> dict:
    key = jax.random.key(seed)
    ks = jax.random.split(key, 32)
    f32 = jnp.float32
    D, F, G, P, C = D_MODEL, FFN_HIDDEN, N_GROUPS, SSM_STATE, SSM_GROUP
    nrm = lambda k, shape, s: jax.random.normal(k, shape, f32) * s
    x = jax.random.normal(ks[0], (BATCH, SEQ, D), f32)
    norm_mix_g = 1.0 + nrm(ks[1], (DEPTH, D), 0.02)
    norm_ffn_g = 1.0 + nrm(ks[2], (DEPTH, D), 0.02)
    ffn_w1 = nrm(ks[3], (DEPTH, D, F), D ** -0.5)
    ffn_w3 = nrm(ks[4], (DEPTH, D, F), D ** -0.5)
    ffn_w2 = nrm(ks[5], (DEPTH, F, D), F ** -0.5)
    ssm_lam_re = -0.5 + nrm(ks[6], (N_A_LAYERS, G, P), 0.01)
    ssm_lam_im = math.pi * jnp.arange(P, dtype=f32)[None, None, :] + nrm(ks[7], (N_A_LAYERS, G, P), 0.01)
    ssm_log_dt = jax.random.uniform(ks[8], (N_A_LAYERS, G), f32, math.log(DT_MIN), math.log(DT_MAX))
    ssm_b_re = nrm(ks[9], (N_A_LAYERS, G, P, C), (0.5 / C) ** 0.5)
    ssm_b_im = nrm(ks[10], (N_A_LAYERS, G, P, C), (0.5 / C) ** 0.5)
    ssm_c_re = nrm(ks[11], (N_A_LAYERS, G, C, P), (0.5 / P) ** 0.5)
    ssm_c_im = nrm(ks[12], (N_A_LAYERS, G, C, P), (0.5 / P) ** 0.5)
    ssm_d = nrm(ks[13], (N_A_LAYERS, D), 1.0)
    ssm_w_glu = nrm(ks[14], (N_A_LAYERS, D, 2 * D), D ** -0.5)
    kv_norm_g = 1.0 + nrm(ks[15], (D,), 0.02)
    w_kv = nrm(ks[16], (D, 2 * N_HEADS * HEAD_DIM + N_HEADS * V_DIM), D ** -0.5)
    attn_w_q = nrm(ks[17], (N_B_LAYERS, D, 2 * N_HEADS * HEAD_DIM), D ** -0.5)
    attn_lq1 = nrm(ks[18], (N_B_LAYERS, HEAD_DIM), LAMBDA_INIT_STD)
    attn_lk1 = nrm(ks[19], (N_B_LAYERS, HEAD_DIM), LAMBDA_INIT_STD)
    attn_lq2 = nrm(ks[20], (N_B_LAYERS, HEAD_DIM), LAMBDA_INIT_STD)
    attn_lk2 = nrm(ks[21], (N_B_LAYERS, HEAD_DIM), LAMBDA_INIT_STD)
    attn_subln_g = 1.0 + nrm(ks[22], (N_B_LAYERS, V_DIM), 0.02)
    attn_w_o = nrm(ks[23], (N_B_LAYERS, N_HEADS * V_DIM, D), (N_HEADS * V_DIM) ** -0.5)
    final_norm_g = 1.0 + nrm(ks[24], (D,), 0.02)
    return {"x": x, "norm_mix_g": norm_mix_g, "norm_ffn_g": norm_ffn_g,
            "ffn_w1": ffn_w1, "ffn_w3": ffn_w3, "ffn_w2": ffn_w2,
            "ssm_lam_re": ssm_lam_re, "ssm_lam_im": ssm_lam_im, "ssm_log_dt": ssm_log_dt,
            "ssm_b_re": ssm_b_re, "ssm_b_im": ssm_b_im, "ssm_c_re": ssm_c_re, "ssm_c_im": ssm_c_im,
            "ssm_d": ssm_d, "ssm_w_glu": ssm_w_glu, "kv_norm_g": kv_norm_g, "w_kv": w_kv,
            "attn_w_q": attn_w_q, "attn_lq1": attn_lq1, "attn_lk1": attn_lk1,
            "attn_lq2": attn_lq2, "attn_lk2": attn_lk2, "attn_subln_g": attn_subln_g,
            "attn_w_o": attn_w_o, "final_norm_g": final_norm_g}


def reference(x, norm_mix_g, norm_ffn_g, ffn_w1, ffn_w3, ffn_w2,
              ssm_lam_re, ssm_lam_im, ssm_log_dt, ssm_b_re, ssm_b_im, ssm_c_re, ssm_c_im,
              ssm_d, ssm_w_glu, kv_norm_g, w_kv,
              attn_w_q, attn_lq1, attn_lk1, attn_lq2, attn_lk2, attn_subln_g, attn_w_o,
              final_norm_g):
    L = x.shape[1]
    pos = jnp.arange(L, dtype=jnp.int32)
    h = x
    k1 = k2 = v = None
    for l in range(DEPTH):
        hn = rmsnorm(h, norm_mix_g[l])
        if l < N_A_LAYERS:
            h = h + s5_mixer(hn, ssm_lam_re[l], ssm_lam_im[l], ssm_log_dt[l],
                             ssm_b_re[l], ssm_b_im[l], ssm_c_re[l], ssm_c_im[l],
                             ssm_d[l], ssm_w_glu[l])
        else:
            if l == N_A_LAYERS:
                k1, k2, v = shared_kv(h, kv_norm_g, w_kv, pos)
            j = l - N_A_LAYERS
            lam_init = 0.8 - 0.6 * math.exp(-0.3 * l)
            h = h + diff_attention(hn, k1, k2, v, attn_w_q[j], attn_lq1[j], attn_lk1[j],
                                   attn_lq2[j], attn_lk2[j], attn_subln_g[j], attn_w_o[j],
                                   lam_init, pos)
        h = h + swiglu(rmsnorm(h, norm_ffn_g[l]), ffn_w1[l], ffn_w3[l], ffn_w2[l])
    return rmsnorm(h, final_norm_g)
```

```python
import functools
import math

import jax
import jax.numpy as jnp
from jax import lax
from jax.experimental import pallas as pl
from jax.experimental.pallas import tpu as pltpu

F32 = jnp.float32
BF16 = jnp.bfloat16

D_MODEL = 1024
CHUNK = 64
SSM_GROUP = 16
N_GROUPS = D_MODEL // SSM_GROUP
SSM_STATE = 64
N_HEADS = 8
HEAD_DIM = 64
V_DIM = 2 * HEAD_DIM
ROT_DIM = HEAD_DIM // 4
ROPE_THETA = 500000.0
FFN_HIDDEN = 2816
EPS = 1e-6
N_A_LAYERS = 1

LANES = 128
S5_CHUNK = 8
N_SLABS = D_MODEL // LANES
GROUPS_PER_SLAB = LANES // SSM_GROUP
SLAB_STATE = GROUPS_PER_SLAB * SSM_STATE
SLAB_W = S5_CHUNK * LANES
FFN_TILE = 256
VMEM_LIMIT = 56 * 1024 * 1024


def _cparams(sem):
    return pltpu.CompilerParams(dimension_semantics=sem, vmem_limit_bytes=VMEM_LIMIT)


def _rms(x, g):
    return x * lax.rsqrt(jnp.mean(x * x, axis=-1, keepdims=True) + EPS) * g


def _s5_weights(lam_re, lam_im, log_dt, b_re, b_im, c_re, c_im):
    hi = lax.Precision.HIGHEST
    lr, li = lam_re.astype(F32), lam_im.astype(F32)
    dt = jnp.exp(log_dt.astype(F32))[:, None]
    mag = jnp.exp(lr * dt)
    ar, ai = mag * jnp.cos(li * dt), mag * jnp.sin(li * dt)
    nr, ni = ar - 1.0, ai
    den = lr * lr + li * li
    fr = (nr * lr + ni * li) / den
    fi = (ni * lr - nr * li) / den
    br, bi = b_re.astype(F32), b_im.astype(F32)
    bbr = fr[..., None] * br - fi[..., None] * bi
    bbi = fr[..., None] * bi + fi[..., None] * br
    cr, ci = c_re.astype(F32), c_im.astype(F32)

    prs, pis = [jnp.ones_like(ar)], [jnp.zeros_like(ar)]
    for _ in range(S5_CHUNK):
        prs.append(prs[-1] * ar - pis[-1] * ai)
        pis.append(prs[-2] * ai + pis[-1] * ar)
    pr = jnp.stack(prs)
    pi = jnp.stack(pis)

    T = S5_CHUNK
    kr = pr[:T, :, :, None] * bbr[None] - pi[:T, :, :, None] * bbi[None]
    ki = pr[:T, :, :, None] * bbi[None] + pi[:T, :, :, None] * bbr[None]
    klag = (jnp.einsum('gdp,kgpc->kgcd', cr, kr, precision=hi)
            - jnp.einsum('gdp,kgpc->kgcd', ci, ki, precision=hi))
    tt = jnp.arange(T)
    lag = tt[None, :] - tt[:, None]
    toe = jnp.where((lag >= 0)[:, :, None, None, None],
                    klag[jnp.clip(lag, 0, T - 1)], 0.0)
    win_r = kr[::-1]
    win_i = ki[::-1]
    p1r, p1i = pr[1:], pi[1:]
    out_r = cr[None] * p1r[:, :, None, :] - ci[None] * p1i[:, :, None, :]
    out_i = -cr[None] * p1i[:, :, None, :] - ci[None] * p1r[:, :, None, :]

    S, Gs = N_SLABS, GROUPS_PER_SLAB
    eye = jnp.eye(Gs, dtype=F32)
    toe_s = toe.reshape(T, T, S, Gs, SSM_GROUP, SSM_GROUP)
    m_toe = jnp.einsum('jtsgcd,gh->sjgcthd', toe_s, eye).reshape(S, SLAB_W, SLAB_W)
    win_rs = win_r.reshape(T, S, Gs, SSM_STATE, SSM_GROUP)
    win_is = win_i.reshape(T, S, Gs, SSM_STATE, SSM_GROUP)
    m_win_r = jnp.einsum('jsgpc,gh->sjgchp', win_rs, eye).reshape(S, SLAB_W, SLAB_STATE)
    m_win_i = jnp.einsum('jsgpc,gh->sjgchp', win_is, eye).reshape(S, SLAB_W, SLAB_STATE)
    w1 = jnp.concatenate([m_toe, m_win_r, m_win_i], axis=-1).astype(BF16)
    out_rs = out_r.reshape(T, S, Gs, SSM_GROUP, SSM_STATE)
    out_is = out_i.reshape(T, S, Gs, SSM_GROUP, SSM_STATE)
    m_out_r = jnp.einsum('tsgdp,gh->sgpthd', out_rs, eye).reshape(S, SLAB_STATE, SLAB_W)
    m_out_i = jnp.einsum('tsgdp,gh->sgpthd', out_is, eye).reshape(S, SLAB_STATE, SLAB_W)
    wout = jnp.concatenate([m_out_r, m_out_i], axis=1).astype(BF16)
    a_chunk = (pr[T].reshape(S, SLAB_STATE), pi[T].reshape(S, SLAB_STATE))
    return w1, wout, a_chunk


def _scan_powers(a_chunk, n_steps):
    r, i = a_chunk
    rows = []
    for _ in range(n_steps):
        rows.append(jnp.stack([r, i], axis=1))
        r, i = r * r - i * i, 2.0 * r * i
    return jnp.stack(rows, axis=1)


def _s5_pre_kernel(x_ref, g_ref, u_ref):
    g = g_ref[...]
    for j in range(S5_CHUNK):
        u = _rms(x_ref[:, j * D_MODEL:(j + 1) * D_MODEL], g).astype(BF16)
        for s in range(N_SLABS):
            u_ref[s, :, j * LANES:(j + 1) * LANES] = u[:, s * LANES:(s + 1) * LANES]


def _shift_rows(x, d, row):
    n = x.shape[0]
    if d % 8 == 0:
        return jnp.concatenate([jnp.zeros((d, x.shape[1]), x.dtype), x[:n - d]], axis=0)
    return jnp.where(row >= d, pltpu.roll(x, d, 0), 0.0)


def _s5_core_kernel(u_ref, w1_ref, wout_ref, pw_ref, y_ref, carry_ref, *, n_steps):
    @pl.when(pl.program_id(1) == 0)
    def _():
        carry_ref[...] = jnp.zeros_like(carry_ref)

    res = jnp.dot(u_ref[0], w1_ref[0], preferred_element_type=F32)
    nb = res.shape[0]
    y_intra = res[:, :SLAB_W]
    er = res[:, SLAB_W:SLAB_W + SLAB_STATE]
    ei = res[:, SLAB_W + SLAB_STATE:]
    row = lax.broadcasted_iota(jnp.int32, (nb, SLAB_STATE), 0)
    c_r = carry_ref[:, :SLAB_STATE]
    c_i = carry_ref[:, SLAB_STATE:]
    a_r, a_i = pw_ref[0, 0, 0:1, :], pw_ref[0, 0, 1:2, :]
    first = row == 0
    er = er + jnp.where(first, a_r * c_r - a_i * c_i, 0.0)
    ei = ei + jnp.where(first, a_r * c_i + a_i * c_r, 0.0)
    for i in range(n_steps):
        d = 1 << i
        p_r, p_i = pw_ref[0, i, 0:1, :], pw_ref[0, i, 1:2, :]
        sr, si = _shift_rows(er, d, row), _shift_rows(ei, d, row)
        er, ei = er + p_r * sr - p_i * si, ei + p_r * si + p_i * sr
    prev_r = jnp.where(first, c_r, pltpu.roll(er, 1, 0))
    prev_i = jnp.where(first, c_i, pltpu.roll(ei, 1, 0))
    carry_ref[:, :SLAB_STATE] = er[nb - 1:nb, :]
    carry_ref[:, SLAB_STATE:] = ei[nb - 1:nb, :]
    prev = jnp.concatenate([prev_r, prev_i], axis=1).astype(BF16)
    y_ref[0] = y_intra + jnp.dot(prev, wout_ref[0], preferred_element_type=F32)


def _s5_post_kernel(x_ref, y_ref, g_ref, d_ref, wglu_ref, o_ref):
    g = g_ref[...]
    dsk = d_ref[...]
    for j in range(S5_CHUNK):
        xj = x_ref[:, j * D_MODEL:(j + 1) * D_MODEL]
        u = _rms(xj, g)
        y = jnp.concatenate([y_ref[s, :, j * LANES:(j + 1) * LANES] for s in range(N_SLABS)], axis=1)
        z = jax.nn.gelu(y + dsk * u, approximate=True).astype(BF16)
        zz = jnp.dot(z, wglu_ref[...], preferred_element_type=F32)
        za, zb = zz[:, :D_MODEL], zz[:, D_MODEL:]
        o_ref[:, j * D_MODEL:(j + 1) * D_MODEL] = xj + za * (1.0 / (1.0 + jnp.exp(-zb)))


def _s5_layer(x2, g_mix, w1, wout, pw, d_skip, w_glu):
    n_rows = x2.shape[0]
    nb = min(128, n_rows)
    nbc = min(256, n_rows)
    n_steps = int(math.log2(nbc))
    assert n_rows % nb == 0 and n_rows % nbc == 0 and (1 << n_steps) == nbc
    row_w = S5_CHUNK * D_MODEL

    u = pl.pallas_call(
        _s5_pre_kernel,
        grid=(n_rows // nb,),
        in_specs=[pl.BlockSpec((nb, row_w), lambda i: (i, 0)),
                  pl.BlockSpec((1, D_MODEL), lambda i: (0, 0))],
        out_specs=pl.BlockSpec((N_SLABS, nb, SLAB_W), lambda i: (0, i, 0)),
        out_shape=jax.ShapeDtypeStruct((N_SLABS, n_rows, SLAB_W), BF16),
        compiler_params=_cparams(("parallel",)),
        name="s5_pre",
    )(x2, g_mix)

    y = pl.pallas_call(
        functools.partial(_s5_core_kernel, n_steps=n_steps),
        grid=(N_SLABS, n_rows // nbc),
        in_specs=[pl.BlockSpec((1, nbc, SLAB_W), lambda s, r: (s, r, 0)),
                  pl.BlockSpec((1, SLAB_W, SLAB_W + 2 * SLAB_STATE), lambda s, r: (s, 0, 0)),
                  pl.BlockSpec((1, 2 * SLAB_STATE, SLAB_W), lambda s, r: (s, 0, 0)),
                  pl.BlockSpec((1, n_steps, 2, SLAB_STATE), lambda s, r: (s, 0, 0, 0))],
        out_specs=pl.BlockSpec((1, nbc, SLAB_W), lambda s, r: (s, r, 0)),
        out_shape=jax.ShapeDtypeStruct((N_SLABS, n_rows, SLAB_W), F32),
        scratch_shapes=[pltpu.VMEM((1, 2 * SLAB_STATE), F32)],
        compiler_params=_cparams(("arbitrary", "arbitrary")),
        name="s5_core",
    )(u, w1, wout, pw)

    return pl.pallas_call(
        _s5_post_kernel,
        grid=(n_rows // nb,),
        in_specs=[pl.BlockSpec((nb, row_w), lambda i: (i, 0)),
                  pl.BlockSpec((N_SLABS, nb, SLAB_W), lambda i: (0, i, 0)),
                  pl.BlockSpec((1, D_MODEL), lambda i: (0, 0)),
                  pl.BlockSpec((1, D_MODEL), lambda i: (0, 0)),
                  pl.BlockSpec((D_MODEL, 2 * D_MODEL), lambda i: (0, 0))],
        out_specs=pl.BlockSpec((nb, row_w), lambda i: (i, 0)),
        out_shape=jax.ShapeDtypeStruct(x2.shape, F32),
        compiler_params=_cparams(("parallel",)),
        name="s5_post",
    )(x2, y, g_mix, d_skip, w_glu)


def _ffn_body(h, g, w1_ref, w3_ref, w2_ref):
    hn = _rms(h, g).astype(BF16)
    acc = jnp.zeros(h.shape, F32)
    for f in range(FFN_HIDDEN // FFN_TILE):
        sl = slice(f * FFN_TILE, (f + 1) * FFN_TILE)
        a = jnp.dot(hn, w1_ref[:, sl], preferred_element_type=F32)
        b = jnp.dot(hn, w3_ref[:, sl], preferred_element_type=F32)
        t = (a * (1.0 / (1.0 + jnp.exp(-a))) * b).astype(BF16)
        acc = acc + jnp.dot(t, w2_ref[sl, :], preferred_element_type=F32)
    return h + acc


def _ffn_kernel(h_ref, g_ref, w1_ref, w3_ref, w2_ref, o_ref):
    o_ref[...] = _ffn_body(h_ref[...], g_ref[...], w1_ref, w3_ref, w2_ref)


def _attn_out_ffn_kernel(h_ref, a_ref, wo_ref, g_ref, w1_ref, w3_ref, w2_ref, gf_ref, o_ref):
    h = h_ref[...] + jnp.dot(a_ref[...], wo_ref[...], preferred_element_type=F32)
    o_ref[...] = _rms(_ffn_body(h, g_ref[...], w1_ref, w3_ref, w2_ref), gf_ref[...])


def _const_spec(shape):
    return pl.BlockSpec(shape, lambda i: (0,) * len(shape), pipeline_mode=pl.Buffered(1))


def _ffn_layer(h, g, w1, w3, w2):
    n, tm = h.shape[0], min(512, h.shape[0])
    return pl.pallas_call(
        _ffn_kernel,
        grid=(n // tm,),
        in_specs=[pl.BlockSpec((tm, D_MODEL), lambda i: (i, 0)),
                  _const_spec((1, D_MODEL)),
                  _const_spec((D_MODEL, FFN_HIDDEN)),
                  _const_spec((D_MODEL, FFN_HIDDEN)),
                  _const_spec((FFN_HIDDEN, D_MODEL))],
        out_specs=pl.BlockSpec((tm, D_MODEL), lambda i: (i, 0)),
        out_shape=jax.ShapeDtypeStruct(h.shape, F32),
        compiler_params=_cparams(("parallel",)),
        name="ffn",
    )(h, g, w1, w3, w2)


def _attn_out_ffn_layer(h, a, wo, g, w1, w3, w2, gf):
    n, tm = h.shape[0], min(512, h.shape[0])
    return pl.pallas_call(
        _attn_out_ffn_kernel,
        grid=(n // tm,),
        in_specs=[pl.BlockSpec((tm, D_MODEL), lambda i: (i, 0)),
                  pl.BlockSpec((tm, D_MODEL), lambda i: (i, 0)),
                  _const_spec((D_MODEL, D_MODEL)),
                  _const_spec((1, D_MODEL)),
                  _const_spec((D_MODEL, FFN_HIDDEN)),
                  _const_spec((D_MODEL, FFN_HIDDEN)),
                  _const_spec((FFN_HIDDEN, D_MODEL)),
                  _const_spec((1, D_MODEL))],
        out_specs=pl.BlockSpec((tm, D_MODEL), lambda i: (i, 0)),
        out_shape=jax.ShapeDtypeStruct(h.shape, F32),
        compiler_params=_cparams(("parallel",)),
        name="attn_out_ffn",
    )(h, a, wo, g, w1, w3, w2, gf)


def _rope(t, cos, sin_lo, sin_hi):
    return t * cos + pltpu.roll(t, LANES - ROT_DIM // 2, 1) * sin_lo + pltpu.roll(t, ROT_DIM // 2, 1) * sin_hi


def _proj_kernel(h_ref, gkv_ref, gq_ref, wkv_ref, wq_ref, cos_ref, slo_ref, shi_ref, kk_ref, v_ref, qq_ref):
    h = h_ref[...]
    cos, slo, shi = cos_ref[...], slo_ref[...], shi_ref[...]
    kv = jnp.dot(_rms(h, gkv_ref[...]).astype(BF16), wkv_ref[...], preferred_element_type=F32)
    q = jnp.dot(_rms(h, gq_ref[...]).astype(BF16), wq_ref[...], preferred_element_type=F32)
    for hd in range(N_HEADS):
        sl = slice(hd * LANES, (hd + 1) * LANES)
        kk_ref[hd] = _rope(kv[:, sl], cos, slo, shi).astype(BF16)
        qq_ref[hd] = _rope(q[:, sl], cos, slo, shi).astype(BF16)
        v_ref[hd] = kv[:, D_MODEL + hd * LANES:D_MODEL + (hd + 1) * LANES].astype(BF16)


def _proj_layer(h, gkv, gq, wkv, wq, cos, slo, shi):
    n, tm = h.shape[0], min(512, h.shape[0])
    hshape = jax.ShapeDtypeStruct((N_HEADS, n, LANES), BF16)
    hspec = pl.BlockSpec((N_HEADS, tm, LANES), lambda i: (0, i, 0))
    tspec = pl.BlockSpec((tm, LANES), lambda i: (i, 0))
    return pl.pallas_call(
        _proj_kernel,
        grid=(n // tm,),
        in_specs=[pl.BlockSpec((tm, D_MODEL), lambda i: (i, 0)),
                  _const_spec((1, D_MODEL)), _const_spec((1, D_MODEL)),
                  _const_spec((D_MODEL, 2 * D_MODEL)), _const_spec((D_MODEL, D_MODEL)),
                  tspec, tspec, tspec],
        out_specs=[hspec, hspec, hspec],
        out_shape=[hshape, hshape, hshape],
        compiler_params=_cparams(("parallel",)),
        name="qkv_proj",
    )(h, gkv, gq, wkv, wq, cos, slo, shi)


def _rope_tables(n):
    half = ROT_DIM // 2
    inv_freq = ROPE_THETA ** (-jnp.arange(half, dtype=F32) * 2.0 / ROT_DIM)
    ang = jnp.arange(n, dtype=jnp.int32).astype(F32)[:, None] * inv_freq[None, :]
    cos, sin = jnp.cos(ang), jnp.sin(ang)
    ones = jnp.ones((n, HEAD_DIM - ROT_DIM), F32)
    zeros = jnp.zeros((n, HEAD_DIM - half), F32)
    cos_h = jnp.concatenate([cos, cos, ones], axis=1)
    lo_h = jnp.concatenate([-sin, zeros], axis=1)
    hi_h = jnp.concatenate([jnp.zeros((n, half), F32), sin, jnp.zeros((n, HEAD_DIM - ROT_DIM), F32)], axis=1)
    two = lambda t: jnp.concatenate([t, t], axis=1)
    return two(cos_h), two(lo_h), two(hi_h)


def _attn_kernel(lam_ref, qq_ref, kk_ref, v_ref, g_ref, o_ref, q2_ref, m_ref, l_ref, acc_ref, *, tq, lam_init):
    qi = pl.program_id(1)
    q = qq_ref[0]
    lane = lax.broadcasted_iota(jnp.int32, q.shape, 1)
    zero = jnp.zeros_like(q)
    q2_ref[:tq, :] = jnp.where(lane < HEAD_DIM, q, zero)
    q2_ref[tq:, :] = jnp.where(lane >= HEAD_DIM, q, zero)
    m_ref[...] = jnp.full(m_ref.shape, -jnp.inf, F32)
    l_ref[...] = jnp.zeros(l_ref.shape, F32)
    acc_ref[...] = jnp.zeros(acc_ref.shape, F32)

    def step(ki, masked):
        k = kk_ref[0, pl.ds(pl.multiple_of(ki * tq, tq), tq), :]
        v = v_ref[0, pl.ds(pl.multiple_of(ki * tq, tq), tq), :]
        s = lax.dot_general(q2_ref[...], k, (((1,), (1,)), ((), ())), preferred_element_type=F32)
        if masked:
            r = lax.broadcasted_iota(jnp.int32, s.shape, 0)
            c = lax.broadcasted_iota(jnp.int32, s.shape, 1)
            r = jnp.where(r >= tq, r - tq, r)
            s = jnp.where((c // CHUNK) <= (r // CHUNK), s, -jnp.inf)
        m_prev = m_ref[...]
        m_new = jnp.maximum(m_prev, jnp.max(s, axis=1, keepdims=True))
        alpha = jnp.exp(m_prev - m_new)
        p = jnp.exp(s - m_new[:, :1])
        l_ref[...] = alpha * l_ref[...] + jnp.sum(p, axis=1, keepdims=True)
        acc_ref[...] = alpha * acc_ref[...] + jnp.dot(p.astype(BF16), v, preferred_element_type=F32)
        m_ref[...] = m_new

    def body(ki, c):
        step(ki, False)
        return c

    lax.fori_loop(0, qi, body, 0)
    step(qi, True)

    o = acc_ref[...] / l_ref[...]
    o = o[:tq] - lam_ref[0] * o[tq:]
    o_ref[...] = (_rms(o, g_ref[...]) * (1.0 - lam_init)).astype(o_ref.dtype)


def _attn_layer(lam, qq, kk, v, g, lam_init):
    n = qq.shape[1]
    tq = min(512, n)
    assert n % tq == 0 and tq % CHUNK == 0
    return pl.pallas_call(
        functools.partial(_attn_kernel, tq=tq, lam_init=lam_init),
        grid=(N_HEADS, n // tq),
        in_specs=[pl.BlockSpec(memory_space=pltpu.SMEM),
                  pl.BlockSpec((1, tq, LANES), lambda h, i: (h, i, 0)),
                  pl.BlockSpec((1, n, LANES), lambda h, i: (h, 0, 0)),
                  pl.BlockSpec((1, n, LANES), lambda h, i: (h, 0, 0)),
                  pl.BlockSpec((1, LANES), lambda h, i: (0, 0))],
        out_specs=pl.BlockSpec((tq, LANES), lambda h, i: (i, h)),
        out_shape=jax.ShapeDtypeStruct((n, N_HEADS * V_DIM), BF16),
        scratch_shapes=[pltpu.VMEM((2 * tq, LANES), BF16),
                        pltpu.VMEM((2 * tq, LANES), F32),
                        pltpu.VMEM((2 * tq, LANES), F32),
                        pltpu.VMEM((2 * tq, LANES), F32)],
        compiler_params=_cparams(("parallel", "arbitrary")),
        name="diff_attn",
    )(lam, qq, kk, v, g)


def _head_pairs(w):
    return w.reshape(D_MODEL, 2, N_HEADS, HEAD_DIM).transpose(0, 2, 1, 3).reshape(D_MODEL, 2 * N_HEADS * HEAD_DIM)


def kernel(x, norm_mix_g, norm_ffn_g, ffn_w1, ffn_w3, ffn_w2, ssm_lam_re, ssm_lam_im, ssm_log_dt, ssm_b_re, ssm_b_im, ssm_c_re, ssm_c_im, ssm_d, ssm_w_glu, kv_norm_g, w_kv, attn_w_q, attn_lq1, attn_lk1, attn_lq2, attn_lk2, attn_subln_g, attn_w_o, final_norm_g):
    bsz, L, _ = x.shape
    assert bsz == 1 and L % (S5_CHUNK * 8) == 0
    row = lambda t: t.reshape(1, -1).astype(F32)

    w1, wout, a_chunk = _s5_weights(ssm_lam_re[0], ssm_lam_im[0], ssm_log_dt[0], ssm_b_re[0], ssm_b_im[0],
                                    ssm_c_re[0], ssm_c_im[0])
    n_rows = L // S5_CHUNK
    pw = _scan_powers(a_chunk, int(math.log2(min(256, n_rows))))
    x2 = x.reshape(n_rows, S5_CHUNK * D_MODEL)
    h = _s5_layer(x2, row(norm_mix_g[0]), w1, wout, pw, row(ssm_d[0]), ssm_w_glu[0].astype(BF16))
    h = h.reshape(L, D_MODEL)
    h = _ffn_layer(h, row(norm_ffn_g[0]), ffn_w1[0].astype(BF16), ffn_w3[0].astype(BF16), ffn_w2[0].astype(BF16))

    hk = N_HEADS * HEAD_DIM
    wkv = jnp.concatenate([_head_pairs(w_kv[:, :2 * hk]), w_kv[:, 2 * hk:]], axis=1).astype(BF16)
    wq = (_head_pairs(attn_w_q[0]) * (HEAD_DIM ** -0.5)).astype(BF16)
    cos, slo, shi = _rope_tables(L)
    kk, v, qq = _proj_layer(h, row(kv_norm_g), row(norm_mix_g[1]), wkv, wq, cos, slo, shi)
    lam_init = 0.8 - 0.6 * math.exp(-0.3 * N_A_LAYERS)
    lam = (jnp.exp(jnp.sum(attn_lq1[0].astype(F32) * attn_lk1[0].astype(F32)))
           - jnp.exp(jnp.sum(attn_lq2[0].astype(F32) * attn_lk2[0].astype(F32))) + lam_init).reshape(1)
    a = _attn_layer(lam, qq, kk, v, row(attn_subln_g[0]), lam_init)
    out = _attn_out_ffn_layer(h, a, attn_w_o[0].astype(BF16), row(norm_ffn_g[1]), ffn_w1[1].astype(BF16),
                              ffn_w3[1].astype(BF16), ffn_w2[1].astype(BF16), row(final_norm_g))
    return out.reshape(1, L, D_MODEL)
```

```python
import functools
import math

import jax
import jax.numpy as jnp
from jax import lax
from jax.experimental import pallas as pl
from jax.experimental.pallas import tpu as pltpu

F32 = jnp.float32
BF16 = jnp.bfloat16

D_MODEL = 1024
CHUNK = 64
SSM_GROUP = 16
N_GROUPS = D_MODEL // SSM_GROUP
SSM_STATE = 64
N_HEADS = 8
HEAD_DIM = 64
V_DIM = 2 * HEAD_DIM
ROT_DIM = HEAD_DIM // 4
ROPE_THETA = 500000.0
FFN_HIDDEN = 2816
EPS = 1e-6
N_A_LAYERS = 1

LANES = 128
S5_CHUNK = 8
N_SLABS = D_MODEL // LANES
GROUPS_PER_SLAB = LANES // SSM_GROUP
SLAB_STATE = GROUPS_PER_SLAB * SSM_STATE
SLAB_W = S5_CHUNK * LANES
FFN_TILE = 256
Q_SCALE = HEAD_DIM ** -0.5 * math.log2(math.e)
VMEM_LIMIT = 56 * 1024 * 1024


def _cparams(sem):
    return pltpu.CompilerParams(dimension_semantics=sem, vmem_limit_bytes=VMEM_LIMIT)


def _rms(x, g):
    return x * lax.rsqrt(jnp.mean(x * x, axis=-1, keepdims=True) + EPS) * g


def _s5_weights(lam_re, lam_im, log_dt, b_re, b_im, c_re, c_im):
    hi = lax.Precision.HIGHEST
    lr, li = lam_re.astype(F32), lam_im.astype(F32)
    dt = jnp.exp(log_dt.astype(F32))[:, None]
    mag = jnp.exp(lr * dt)
    ar, ai = mag * jnp.cos(li * dt), mag * jnp.sin(li * dt)
    nr, ni = ar - 1.0, ai
    den = lr * lr + li * li
    fr = (nr * lr + ni * li) / den
    fi = (ni * lr - nr * li) / den
    br, bi = b_re.astype(F32), b_im.astype(F32)
    bbr = fr[..., None] * br - fi[..., None] * bi
    bbi = fr[..., None] * bi + fi[..., None] * br
    cr, ci = c_re.astype(F32), c_im.astype(F32)

    prs, pis = [jnp.ones_like(ar)], [jnp.zeros_like(ar)]
    for _ in range(S5_CHUNK):
        prs.append(prs[-1] * ar - pis[-1] * ai)
        pis.append(prs[-2] * ai + pis[-1] * ar)
    pr = jnp.stack(prs)
    pi = jnp.stack(pis)

    T = S5_CHUNK
    kr = pr[:T, :, :, None] * bbr[None] - pi[:T, :, :, None] * bbi[None]
    ki = pr[:T, :, :, None] * bbi[None] + pi[:T, :, :, None] * bbr[None]
    klag = (jnp.einsum('gdp,kgpc->kgcd', cr, kr, precision=hi)
            - jnp.einsum('gdp,kgpc->kgcd', ci, ki, precision=hi))
    tt = jnp.arange(T)
    lag = tt[None, :] - tt[:, None]
    toe = jnp.where((lag >= 0)[:, :, None, None, None],
                    klag[jnp.clip(lag, 0, T - 1)], 0.0)
    win_r = kr[::-1]
    win_i = ki[::-1]
    p1r, p1i = pr[1:], pi[1:]
    out_r = cr[None] * p1r[:, :, None, :] - ci[None] * p1i[:, :, None, :]
    out_i = -cr[None] * p1i[:, :, None, :] - ci[None] * p1r[:, :, None, :]

    S, Gs = N_SLABS, GROUPS_PER_SLAB
    eye = jnp.eye(Gs, dtype=F32)
    toe_s = toe.reshape(T, T, S, Gs, SSM_GROUP, SSM_GROUP)
    m_toe = jnp.einsum('jtsgcd,gh->sjgcthd', toe_s, eye).reshape(S, SLAB_W, SLAB_W)
    win_rs = win_r.reshape(T, S, Gs, SSM_STATE, SSM_GROUP)
    win_is = win_i.reshape(T, S, Gs, SSM_STATE, SSM_GROUP)
    m_win_r = jnp.einsum('jsgpc,gh->sjgchp', win_rs, eye).reshape(S, SLAB_W, SLAB_STATE)
    m_win_i = jnp.einsum('jsgpc,gh->sjgchp', win_is, eye).reshape(S, SLAB_W, SLAB_STATE)
    w1 = jnp.concatenate([m_toe, m_win_r, m_win_i], axis=-1).astype(BF16)
    out_rs = out_r.reshape(T, S, Gs, SSM_GROUP, SSM_STATE)
    out_is = out_i.reshape(T, S, Gs, SSM_GROUP, SSM_STATE)
    m_out_r = jnp.einsum('tsgdp,gh->sgpthd', out_rs, eye).reshape(S, SLAB_STATE, SLAB_W)
    m_out_i = jnp.einsum('tsgdp,gh->sgpthd', out_is, eye).reshape(S, SLAB_STATE, SLAB_W)
    wout = jnp.concatenate([m_out_r, m_out_i], axis=1).astype(BF16)
    a_chunk = (pr[T].reshape(S, SLAB_STATE), pi[T].reshape(S, SLAB_STATE))
    return w1, wout, a_chunk


def _scan_powers(a_chunk, n_steps):
    r, i = a_chunk
    rows = []
    for _ in range(n_steps):
        rows.append(jnp.stack([r, i], axis=1))
        r, i = r * r - i * i, 2.0 * r * i
    return jnp.stack(rows, axis=1)


def _s5_pre_kernel(x_ref, g_ref, u_ref):
    g = g_ref[...]
    for j in range(S5_CHUNK):
        u = _rms(x_ref[:, j * D_MODEL:(j + 1) * D_MODEL], g).astype(BF16)
        for s in range(N_SLABS):
            u_ref[s, :, j * LANES:(j + 1) * LANES] = u[:, s * LANES:(s + 1) * LANES]


def _shift_rows(x, d, row):
    n = x.shape[0]
    if d % 8 == 0:
        return jnp.concatenate([jnp.zeros((d, x.shape[1]), x.dtype), x[:n - d]], axis=0)
    return jnp.where(row >= d, pltpu.roll(x, d, 0), 0.0)


def _s5_core_kernel(u_ref, w1_ref, wout_ref, pw_ref, y_ref, carry_ref, *, n_steps):
    @pl.when(pl.program_id(1) == 0)
    def _():
        carry_ref[...] = jnp.zeros_like(carry_ref)

    res = jnp.dot(u_ref[0], w1_ref[0], preferred_element_type=F32)
    nb = res.shape[0]
    y_intra = res[:, :SLAB_W]
    er = res[:, SLAB_W:SLAB_W + SLAB_STATE]
    ei = res[:, SLAB_W + SLAB_STATE:]
    row = lax.broadcasted_iota(jnp.int32, (nb, SLAB_STATE), 0)
    c_r = carry_ref[:, :SLAB_STATE]
    c_i = carry_ref[:, SLAB_STATE:]
    a_r, a_i = pw_ref[0, 0, 0:1, :], pw_ref[0, 0, 1:2, :]
    first = row == 0
    er = er + jnp.where(first, a_r * c_r - a_i * c_i, 0.0)
    ei = ei + jnp.where(first, a_r * c_i + a_i * c_r, 0.0)
    for i in range(n_steps):
        d = 1 << i
        p_r, p_i = pw_ref[0, i, 0:1, :], pw_ref[0, i, 1:2, :]
        sr, si = _shift_rows(er, d, row), _shift_rows(ei, d, row)
        er, ei = er + p_r * sr - p_i * si, ei + p_r * si + p_i * sr
    prev_r = jnp.where(first, c_r, pltpu.roll(er, 1, 0))
    prev_i = jnp.where(first, c_i, pltpu.roll(ei, 1, 0))
    carry_ref[:, :SLAB_STATE] = er[nb - 1:nb, :]
    carry_ref[:, SLAB_STATE:] = ei[nb - 1:nb, :]
    prev = jnp.concatenate([prev_r, prev_i], axis=1).astype(BF16)
    y_ref[0] = y_intra + jnp.dot(prev, wout_ref[0], preferred_element_type=F32)


def _s5_post_kernel(x_ref, y_ref, g_ref, d_ref, wglu_ref, o_ref):
    g = g_ref[...]
    dsk = d_ref[...]
    for j in range(S5_CHUNK):
        xj = x_ref[:, j * D_MODEL:(j + 1) * D_MODEL]
        u = _rms(xj, g)
        y = jnp.concatenate([y_ref[s, :, j * LANES:(j + 1) * LANES] for s in range(N_SLABS)], axis=1)
        z = jax.nn.gelu(y + dsk * u, approximate=True).astype(BF16)
        zz = jnp.dot(z, wglu_ref[...], preferred_element_type=F32)
        za, zb = zz[:, :D_MODEL], zz[:, D_MODEL:]
        o_ref[:, j * D_MODEL:(j + 1) * D_MODEL] = xj + za * (1.0 / (1.0 + jnp.exp(-zb)))


def _s5_layer(x2, g_mix, w1, wout, pw, d_skip, w_glu):
    n_rows = x2.shape[0]
    nb = min(128, n_rows)
    nbc = min(256, n_rows)
    n_steps = int(math.log2(nbc))
    assert n_rows % nb == 0 and n_rows % nbc == 0 and (1 << n_steps) == nbc
    row_w = S5_CHUNK * D_MODEL

    u = pl.pallas_call(
        _s5_pre_kernel,
        grid=(n_rows // nb,),
        in_specs=[pl.BlockSpec((nb, row_w), lambda i: (i, 0)),
                  pl.BlockSpec((1, D_MODEL), lambda i: (0, 0))],
        out_specs=pl.BlockSpec((N_SLABS, nb, SLAB_W), lambda i: (0, i, 0)),
        out_shape=jax.ShapeDtypeStruct((N_SLABS, n_rows, SLAB_W), BF16),
        compiler_params=_cparams(("parallel",)),
        name="s5_pre",
    )(x2, g_mix)

    y = pl.pallas_call(
        functools.partial(_s5_core_kernel, n_steps=n_steps),
        grid=(N_SLABS, n_rows // nbc),
        in_specs=[pl.BlockSpec((1, nbc, SLAB_W), lambda s, r: (s, r, 0)),
                  pl.BlockSpec((1, SLAB_W, SLAB_W + 2 * SLAB_STATE), lambda s, r: (s, 0, 0)),
                  pl.BlockSpec((1, 2 * SLAB_STATE, SLAB_W), lambda s, r: (s, 0, 0)),
                  pl.BlockSpec((1, n_steps, 2, SLAB_STATE), lambda s, r: (s, 0, 0, 0))],
        out_specs=pl.BlockSpec((1, nbc, SLAB_W), lambda s, r: (s, r, 0)),
        out_shape=jax.ShapeDtypeStruct((N_SLABS, n_rows, SLAB_W), F32),
        scratch_shapes=[pltpu.VMEM((1, 2 * SLAB_STATE), F32)],
        compiler_params=_cparams(("arbitrary", "arbitrary")),
        name="s5_core",
    )(u, w1, wout, pw)

    return pl.pallas_call(
        _s5_post_kernel,
        grid=(n_rows // nb,),
        in_specs=[pl.BlockSpec((nb, row_w), lambda i: (i, 0)),
                  pl.BlockSpec((N_SLABS, nb, SLAB_W), lambda i: (0, i, 0)),
                  pl.BlockSpec((1, D_MODEL), lambda i: (0, 0)),
                  pl.BlockSpec((1, D_MODEL), lambda i: (0, 0)),
                  pl.BlockSpec((D_MODEL, 2 * D_MODEL), lambda i: (0, 0))],
        out_specs=pl.BlockSpec((nb, row_w), lambda i: (i, 0)),
        out_shape=jax.ShapeDtypeStruct(x2.shape, F32),
        compiler_params=_cparams(("parallel",)),
        name="s5_post",
    )(x2, y, g_mix, d_skip, w_glu)


def _ffn_body(h, g, w1_ref, w3_ref, w2_ref):
    hn = _rms(h, g).astype(BF16)
    acc = jnp.zeros(h.shape, F32)
    for f in range(FFN_HIDDEN // FFN_TILE):
        sl = slice(f * FFN_TILE, (f + 1) * FFN_TILE)
        a = jnp.dot(hn, w1_ref[:, sl], preferred_element_type=F32)
        b = jnp.dot(hn, w3_ref[:, sl], preferred_element_type=F32)
        t = (a * (1.0 / (1.0 + jnp.exp(-a))) * b).astype(BF16)
        acc = acc + jnp.dot(t, w2_ref[sl, :], preferred_element_type=F32)
    return h + acc


def _ffn_kernel(h_ref, g_ref, w1_ref, w3_ref, w2_ref, o_ref):
    o_ref[...] = _ffn_body(h_ref[...], g_ref[...], w1_ref, w3_ref, w2_ref)


def _attn_out_ffn_kernel(h_ref, a_ref, wo_ref, g_ref, w1_ref, w3_ref, w2_ref, gf_ref, o_ref):
    h = h_ref[...] + jnp.dot(a_ref[...], wo_ref[...], preferred_element_type=F32)
    o_ref[...] = _rms(_ffn_body(h, g_ref[...], w1_ref, w3_ref, w2_ref), gf_ref[...])


def _const_spec(shape):
    return pl.BlockSpec(shape, lambda i: (0,) * len(shape), pipeline_mode=pl.Buffered(1))


def _ffn_layer(h, g, w1, w3, w2):
    n, tm = h.shape[0], min(512, h.shape[0])
    return pl.pallas_call(
        _ffn_kernel,
        grid=(n // tm,),
        in_specs=[pl.BlockSpec((tm, D_MODEL), lambda i: (i, 0)),
                  _const_spec((1, D_MODEL)),
                  _const_spec((D_MODEL, FFN_HIDDEN)),
                  _const_spec((D_MODEL, FFN_HIDDEN)),
                  _const_spec((FFN_HIDDEN, D_MODEL))],
        out_specs=pl.BlockSpec((tm, D_MODEL), lambda i: (i, 0)),
        out_shape=jax.ShapeDtypeStruct(h.shape, F32),
        compiler_params=_cparams(("parallel",)),
        name="ffn",
    )(h, g, w1, w3, w2)


def _attn_out_ffn_layer(h, a, wo, g, w1, w3, w2, gf):
    n, tm = h.shape[0], min(512, h.shape[0])
    return pl.pallas_call(
        _attn_out_ffn_kernel,
        grid=(n // tm,),
        in_specs=[pl.BlockSpec((tm, D_MODEL), lambda i: (i, 0)),
                  pl.BlockSpec((tm, D_MODEL), lambda i: (i, 0)),
                  _const_spec((D_MODEL, D_MODEL)),
                  _const_spec((1, D_MODEL)),
                  _const_spec((D_MODEL, FFN_HIDDEN)),
                  _const_spec((D_MODEL, FFN_HIDDEN)),
                  _const_spec((FFN_HIDDEN, D_MODEL)),
                  _const_spec((1, D_MODEL))],
        out_specs=pl.BlockSpec((tm, D_MODEL), lambda i: (i, 0)),
        out_shape=jax.ShapeDtypeStruct(h.shape, F32),
        compiler_params=_cparams(("parallel",)),
        name="attn_out_ffn",
    )(h, a, wo, g, w1, w3, w2, gf)


def _rope(t, cos, sin_lo, sin_hi):
    return t * cos + pltpu.roll(t, LANES - ROT_DIM // 2, 1) * sin_lo + pltpu.roll(t, ROT_DIM // 2, 1) * sin_hi


def _proj_kernel(h_ref, gkv_ref, gq_ref, wkv_ref, wq_ref, cos_ref, slo_ref, shi_ref, kk_ref, v_ref, qq_ref):
    h = h_ref[...]
    cos, slo, shi = cos_ref[...], slo_ref[...], shi_ref[...]
    kv = jnp.dot(_rms(h, gkv_ref[...]).astype(BF16), wkv_ref[...], preferred_element_type=F32)
    q = jnp.dot(_rms(h, gq_ref[...]).astype(BF16), wq_ref[...], preferred_element_type=F32)
    for hd in range(N_HEADS):
        sl = slice(hd * LANES, (hd + 1) * LANES)
        kk_ref[hd] = _rope(kv[:, sl], cos, slo, shi).astype(BF16)
        qq_ref[hd] = (_rope(q[:, sl], cos, slo, shi) * Q_SCALE).astype(BF16)
        v_ref[hd] = kv[:, D_MODEL + hd * LANES:D_MODEL + (hd + 1) * LANES].astype(BF16)


def _proj_layer(h, gkv, gq, wkv, wq, cos, slo, shi):
    n, tm = h.shape[0], min(512, h.shape[0])
    hshape = jax.ShapeDtypeStruct((N_HEADS, n, LANES), BF16)
    hspec = pl.BlockSpec((N_HEADS, tm, LANES), lambda i: (0, i, 0))
    tspec = pl.BlockSpec((tm, LANES), lambda i: (i, 0))
    return pl.pallas_call(
        _proj_kernel,
        grid=(n // tm,),
        in_specs=[pl.BlockSpec((tm, D_MODEL), lambda i: (i, 0)),
                  _const_spec((1, D_MODEL)), _const_spec((1, D_MODEL)),
                  _const_spec((D_MODEL, 2 * D_MODEL)), _const_spec((D_MODEL, D_MODEL)),
                  tspec, tspec, tspec],
        out_specs=[hspec, hspec, hspec],
        out_shape=[hshape, hshape, hshape],
        compiler_params=_cparams(("parallel",)),
        name="qkv_proj",
    )(h, gkv, gq, wkv, wq, cos, slo, shi)


def _rope_tables(n):
    half = ROT_DIM // 2
    inv_freq = ROPE_THETA ** (-jnp.arange(half, dtype=F32) * 2.0 / ROT_DIM)
    ang = jnp.arange(n, dtype=jnp.int32).astype(F32)[:, None] * inv_freq[None, :]
    cos, sin = jnp.cos(ang), jnp.sin(ang)
    ones = jnp.ones((n, HEAD_DIM - ROT_DIM), F32)
    zeros = jnp.zeros((n, HEAD_DIM - half), F32)
    cos_h = jnp.concatenate([cos, cos, ones], axis=1)
    lo_h = jnp.concatenate([-sin, zeros], axis=1)
    hi_h = jnp.concatenate([jnp.zeros((n, half), F32), sin, jnp.zeros((n, HEAD_DIM - ROT_DIM), F32)], axis=1)
    two = lambda t: jnp.concatenate([t, t], axis=1)
    return two(cos_h), two(lo_h), two(hi_h)


def _attn_kernel(lam_ref, qq_ref, kk_ref, v_ref, g_ref, o_ref, q2_ref, sa_ref, sb_ref, ma_ref, mb_ref,
                 ala_ref, alb_ref, mrun_ref, acco_ref, accl_ref, *, tq, tk, lam_init):
    qi = pl.program_id(1)
    q = qq_ref[0]
    lane = lax.broadcasted_iota(jnp.int32, q.shape, 1)
    zero = jnp.zeros_like(q)
    q2_ref[:tq, :] = jnp.where(lane < HEAD_DIM, q, zero)
    q2_ref[tq:, :] = jnp.where(lane >= HEAD_DIM, q, zero)
    mrun_ref[...] = jnp.full(mrun_ref.shape, -jnp.inf, F32)
    acco_ref[...] = jnp.zeros(acco_ref.shape, F32)
    accl_ref[...] = jnp.zeros(accl_ref.shape, F32)
    buf_a = (sa_ref, ma_ref, ala_ref)
    buf_b = (sb_ref, mb_ref, alb_ref)

    def scores(blk, buf, mask_off):
        s_ref, m_ref, al_ref = buf
        k = kk_ref[0, pl.ds(pl.multiple_of(blk * tk, tk), tk), :]
        s = lax.dot_general(q2_ref[...], k, (((1,), (1,)), ((), ())), preferred_element_type=F32)
        if mask_off is not None:
            r = lax.broadcasted_iota(jnp.int32, s.shape, 0)
            c = lax.broadcasted_iota(jnp.int32, s.shape, 1)
            r = jnp.where(r >= tq, r - tq, r)
            s = jnp.where(((c + mask_off) // CHUNK) <= (r // CHUNK), s, -jnp.inf)
        s_ref[...] = s
        m_prev = mrun_ref[...]
        m_new = jnp.maximum(m_prev, jnp.max(s, axis=1, keepdims=True))
        al_ref[...] = jnp.exp2(m_prev - m_new)
        m_ref[...] = m_new
        mrun_ref[...] = m_new

    def accumulate(blk, buf):
        s_ref, m_ref, al_ref = buf
        v = v_ref[0, pl.ds(pl.multiple_of(blk * tk, tk), tk), :]
        v1 = jnp.concatenate([v, jnp.ones_like(v)], axis=1)
        m = m_ref[...]
        p = jnp.concatenate([jnp.exp2(s_ref[:, j * LANES:(j + 1) * LANES] - m) for j in range(tk // LANES)],
                            axis=1).astype(BF16)
        pv = jnp.dot(p, v1, preferred_element_type=F32)
        al = al_ref[...]
        acco_ref[...] = al * acco_ref[...] + pv[:, :LANES]
        accl_ref[...] = al * accl_ref[...] + pv[:, LANES:]

    @pl.when(qi == 0)
    def _():
        scores(0, buf_a, 0)

    @pl.when(qi > 0)
    def _():
        scores(0, buf_a, None)

    def pair(t, next_mask):
        scores(2 * t + 1, buf_b, None)
        accumulate(2 * t, buf_a)
        scores(2 * t + 2, buf_a, next_mask)
        accumulate(2 * t + 1, buf_b)

    def body(t, c):
        pair(t, None)
        return c

    lax.fori_loop(0, qi - 1, body, 0)

    @pl.when(qi > 0)
    def _():
        pair(qi - 1, 0)

    scores(2 * qi + 1, buf_b, tk)
    accumulate(2 * qi, buf_a)
    accumulate(2 * qi + 1, buf_b)

    o = acco_ref[...] / accl_ref[...]
    o = o[:tq] - lam_ref[0] * o[tq:]
    o_ref[...] = (_rms(o, g_ref[...]) * (1.0 - lam_init)).astype(o_ref.dtype)


def _attn_layer(lam, qq, kk, v, g, lam_init):
    n = qq.shape[1]
    tq = min(1024, n)
    tk = tq // 2
    assert n % tq == 0 and tk % CHUNK == 0 and tk % LANES == 0
    stat = pltpu.VMEM((2 * tq, LANES), F32)
    return pl.pallas_call(
        functools.partial(_attn_kernel, tq=tq, tk=tk, lam_init=lam_init),
        grid=(N_HEADS, n // tq),
        in_specs=[pl.BlockSpec(memory_space=pltpu.SMEM),
                  pl.BlockSpec((1, tq, LANES), lambda h, i: (h, i, 0)),
                  pl.BlockSpec((1, n, LANES), lambda h, i: (h, 0, 0)),
                  pl.BlockSpec((1, n, LANES), lambda h, i: (h, 0, 0)),
                  pl.BlockSpec((1, LANES), lambda h, i: (0, 0))],
        out_specs=pl.BlockSpec((tq, LANES), lambda h, i: (i, h)),
        out_shape=jax.ShapeDtypeStruct((n, N_HEADS * V_DIM), BF16),
        scratch_shapes=[pltpu.VMEM((2 * tq, LANES), BF16),
                        pltpu.VMEM((2 * tq, tk), F32), pltpu.VMEM((2 * tq, tk), F32),
                        stat, stat, stat, stat, stat, stat, stat],
        compiler_params=_cparams(("parallel", "arbitrary")),
        name="diff_attn",
    )(lam, qq, kk, v, g)


def _head_pairs(w):
    return w.reshape(D_MODEL, 2, N_HEADS, HEAD_DIM).transpose(0, 2, 1, 3).reshape(D_MODEL, 2 * N_HEADS * HEAD_DIM)


def kernel(x, norm_mix_g, norm_ffn_g, ffn_w1, ffn_w3, ffn_w2, ssm_lam_re, ssm_lam_im, ssm_log_dt, ssm_b_re, ssm_b_im, ssm_c_re, ssm_c_im, ssm_d, ssm_w_glu, kv_norm_g, w_kv, attn_w_q, attn_lq1, attn_lk1, attn_lq2, attn_lk2, attn_subln_g, attn_w_o, final_norm_g):
    bsz, L, _ = x.shape
    assert bsz == 1 and L % (S5_CHUNK * 8) == 0
    row = lambda t: t.reshape(1, -1).astype(F32)

    w1, wout, a_chunk = _s5_weights(ssm_lam_re[0], ssm_lam_im[0], ssm_log_dt[0], ssm_b_re[0], ssm_b_im[0],
                                    ssm_c_re[0], ssm_c_im[0])
    n_rows = L // S5_CHUNK
    pw = _scan_powers(a_chunk, int(math.log2(min(256, n_rows))))
    x2 = x.reshape(n_rows, S5_CHUNK * D_MODEL)
    h = _s5_layer(x2, row(norm_mix_g[0]), w1, wout, pw, row(ssm_d[0]), ssm_w_glu[0].astype(BF16))
    h = h.reshape(L, D_MODEL)
    h = _ffn_layer(h, row(norm_ffn_g[0]), ffn_w1[0].astype(BF16), ffn_w3[0].astype(BF16), ffn_w2[0].astype(BF16))

    hk = N_HEADS * HEAD_DIM
    wkv = jnp.concatenate([_head_pairs(w_kv[:, :2 * hk]), w_kv[:, 2 * hk:]], axis=1).astype(BF16)
    wq = _head_pairs(attn_w_q[0]).astype(BF16)
    cos, slo, shi = _rope_tables(L)
    kk, v, qq = _proj_layer(h, row(kv_norm_g), row(norm_mix_g[1]), wkv, wq, cos, slo, shi)
    lam_init = 0.8 - 0.6 * math.exp(-0.3 * N_A_LAYERS)
    lam = (jnp.exp(jnp.sum(attn_lq1[0].astype(F32) * attn_lk1[0].astype(F32)))
           - jnp.exp(jnp.sum(attn_lq2[0].astype(F32) * attn_lk2[0].astype(F32))) + lam_init).reshape(1)
    a = _attn_layer(lam, qq, kk, v, row(attn_subln_g[0]), lam_init)
    out = _attn_out_ffn_layer(h, a, attn_w_o[0].astype(BF16), row(norm_ffn_g[1]), ffn_w1[1].astype(BF16),
                              ffn_w3[1].astype(BF16), ffn_w2[1].astype(BF16), row(final_norm_g))
    return out.reshape(1, L, D_MODEL)
```

```python
import functools
import math

import jax
import jax.numpy as jnp
import numpy as np
from jax import lax
from jax.experimental import pallas as pl
from jax.experimental.pallas import tpu as pltpu

F32 = jnp.float32
BF16 = jnp.bfloat16

D_MODEL = 1024
CHUNK = 64
SSM_GROUP = 16
N_GROUPS = D_MODEL // SSM_GROUP
SSM_STATE = 64
N_HEADS = 8
HEAD_DIM = 64
V_DIM = 2 * HEAD_DIM
ROT_DIM = HEAD_DIM // 4
ROPE_THETA = 500000.0
FFN_HIDDEN = 2816
EPS = 1e-6
N_A_LAYERS = 1

LANES = 128
S5_CHUNK = 8
N_SLABS = D_MODEL // LANES
GROUPS_PER_SLAB = LANES // SSM_GROUP
SLAB_STATE = GROUPS_PER_SLAB * SSM_STATE
SLAB_W = S5_CHUNK * LANES
FFN_TILE = 256
Q_SCALE = HEAD_DIM ** -0.5 * math.log2(math.e)
VMEM_LIMIT = 56 * 1024 * 1024


def _cparams(sem):
    return pltpu.CompilerParams(dimension_semantics=sem, vmem_limit_bytes=VMEM_LIMIT)


def _rms(x, g):
    return x * lax.rsqrt(jnp.mean(x * x, axis=-1, keepdims=True) + EPS) * g


def _s5_weights(lam_re, lam_im, log_dt, b_re, b_im, c_re, c_im):
    hi = lax.Precision.HIGHEST
    lr, li = lam_re.astype(F32), lam_im.astype(F32)
    dt = jnp.exp(log_dt.astype(F32))[:, None]
    mag = jnp.exp(lr * dt)
    ar, ai = mag * jnp.cos(li * dt), mag * jnp.sin(li * dt)
    nr, ni = ar - 1.0, ai
    den = lr * lr + li * li
    fr = (nr * lr + ni * li) / den
    fi = (ni * lr - nr * li) / den
    br, bi = b_re.astype(F32), b_im.astype(F32)
    bbr = fr[..., None] * br - fi[..., None] * bi
    bbi = fr[..., None] * bi + fi[..., None] * br
    cr, ci = c_re.astype(F32), c_im.astype(F32)

    prs, pis = [jnp.ones_like(ar)], [jnp.zeros_like(ar)]
    for _ in range(S5_CHUNK):
        prs.append(prs[-1] * ar - pis[-1] * ai)
        pis.append(prs[-2] * ai + pis[-1] * ar)
    pr = jnp.stack(prs)
    pi = jnp.stack(pis)

    T = S5_CHUNK
    kr = pr[:T, :, :, None] * bbr[None] - pi[:T, :, :, None] * bbi[None]
    ki = pr[:T, :, :, None] * bbi[None] + pi[:T, :, :, None] * bbr[None]
    klag = (jnp.einsum('gdp,kgpc->kgcd', cr, kr, precision=hi)
            - jnp.einsum('gdp,kgpc->kgcd', ci, ki, precision=hi))
    tt = jnp.arange(T)
    lag = tt[None, :] - tt[:, None]
    toe = jnp.where((lag >= 0)[:, :, None, None, None],
                    klag[jnp.clip(lag, 0, T - 1)], 0.0)
    win_r = kr[::-1]
    win_i = ki[::-1]
    p1r, p1i = pr[1:], pi[1:]
    out_r = cr[None] * p1r[:, :, None, :] - ci[None] * p1i[:, :, None, :]
    out_i = -cr[None] * p1i[:, :, None, :] - ci[None] * p1r[:, :, None, :]

    S, Gs = N_SLABS, GROUPS_PER_SLAB
    c_toe = toe.reshape(T, T, S, Gs, SSM_GROUP, SSM_GROUP).transpose(2, 0, 3, 4, 1, 5).reshape(S, SLAB_W, LANES)
    to_rows = lambda w: w.reshape(T, S, Gs, SSM_STATE, SSM_GROUP).transpose(1, 0, 2, 4, 3).reshape(S, SLAB_W, SSM_STATE)
    cw1 = jnp.concatenate([c_toe, to_rows(win_r), to_rows(win_i)], axis=-1).astype(BF16)
    to_out = lambda w: w.reshape(T, S, Gs, SSM_GROUP, SSM_STATE).transpose(1, 2, 4, 0, 3).reshape(S, SLAB_STATE, LANES)
    cwout = jnp.concatenate([to_out(out_r), to_out(out_i)], axis=1).astype(BF16)
    a_chunk = (pr[T].reshape(S, SLAB_STATE), pi[T].reshape(S, SLAB_STATE))
    return cw1, cwout, a_chunk


def _expansion_matrix():
    e = np.zeros((2 * LANES, SLAB_W + 2 * SLAB_STATE), np.float32)
    for h in range(GROUPS_PER_SLAB):
        for t in range(S5_CHUNK):
            for d in range(SSM_GROUP):
                e[t * SSM_GROUP + d, t * LANES + h * SSM_GROUP + d] = 1.0
        for p in range(SSM_STATE):
            e[LANES + p, SLAB_W + h * SSM_STATE + p] = 1.0
            e[LANES + SSM_STATE + p, SLAB_W + SLAB_STATE + h * SSM_STATE + p] = 1.0
    return jnp.asarray(e, BF16)


def _scan_powers(a_chunk, n_steps):
    r, i = a_chunk
    rows = []
    for _ in range(n_steps):
        rows.append(jnp.stack([r, i], axis=1))
        r, i = r * r - i * i, 2.0 * r * i
    return jnp.stack(rows, axis=1)


def _s5_pre_kernel(x_ref, g_ref, u_ref):
    g = g_ref[...]
    for j in range(S5_CHUNK):
        u = _rms(x_ref[:, j * D_MODEL:(j + 1) * D_MODEL], g).astype(BF16)
        for s in range(N_SLABS):
            u_ref[s, :, j * LANES:(j + 1) * LANES] = u[:, s * LANES:(s + 1) * LANES]


def _shift_rows(x, d, row):
    n = x.shape[0]
    if d % 8 == 0:
        return jnp.concatenate([jnp.zeros((d, x.shape[1]), x.dtype), x[:n - d]], axis=0)
    return jnp.where(row >= d, pltpu.roll(x, d, 0), 0.0)


def _expand_block_diag(c, e_ref, w_ref, row_period, row_group):
    n_rows = c.shape[0]
    tile = SLAB_STATE
    for ct in range(w_ref.shape[1] // tile):
        x = jnp.dot(c, e_ref[:c.shape[1], ct * tile:(ct + 1) * tile], preferred_element_type=F32)
        row = lax.broadcasted_iota(jnp.int32, (n_rows, tile), 0)
        col = lax.broadcasted_iota(jnp.int32, (n_rows, tile), 1)
        if ct * tile < SLAB_W:
            col_g = (col % LANES) // SSM_GROUP
        else:
            col_g = col // SSM_STATE
        keep = (row % row_period) // row_group == col_g
        w_ref[:, ct * tile:(ct + 1) * tile] = jnp.where(keep, x, 0.0).astype(w_ref.dtype)


def _s5_core_kernel(u_ref, cw1_ref, cwout_ref, e_ref, pw_ref, y_ref, carry_ref, w1_ref, wout_ref, *, n_steps):
    @pl.when(pl.program_id(1) == 0)
    def _():
        carry_ref[...] = jnp.zeros_like(carry_ref)
        _expand_block_diag(cw1_ref[0], e_ref, w1_ref, LANES, SSM_GROUP)
        _expand_block_diag(cwout_ref[0], e_ref, wout_ref, SLAB_STATE, SSM_STATE)

    res = jnp.dot(u_ref[0], w1_ref[...], preferred_element_type=F32)
    nb = res.shape[0]
    y_intra = res[:, :SLAB_W]
    er = res[:, SLAB_W:SLAB_W + SLAB_STATE]
    ei = res[:, SLAB_W + SLAB_STATE:]
    row = lax.broadcasted_iota(jnp.int32, (nb, SLAB_STATE), 0)
    c_r = carry_ref[:, :SLAB_STATE]
    c_i = carry_ref[:, SLAB_STATE:]
    a_r, a_i = pw_ref[0, 0, 0:1, :], pw_ref[0, 0, 1:2, :]
    first = row == 0
    er = er + jnp.where(first, a_r * c_r - a_i * c_i, 0.0)
    ei = ei + jnp.where(first, a_r * c_i + a_i * c_r, 0.0)
    for i in range(n_steps):
        d = 1 << i
        p_r, p_i = pw_ref[0, i, 0:1, :], pw_ref[0, i, 1:2, :]
        sr, si = _shift_rows(er, d, row), _shift_rows(ei, d, row)
        er, ei = er + p_r * sr - p_i * si, ei + p_r * si + p_i * sr
    prev_r = jnp.where(first, c_r, pltpu.roll(er, 1, 0))
    prev_i = jnp.where(first, c_i, pltpu.roll(ei, 1, 0))
    carry_ref[:, :SLAB_STATE] = er[nb - 1:nb, :]
    carry_ref[:, SLAB_STATE:] = ei[nb - 1:nb, :]
    prev = jnp.concatenate([prev_r, prev_i], axis=1).astype(BF16)
    y_ref[0] = y_intra + jnp.dot(prev, wout_ref[...], preferred_element_type=F32)


def _s5_post_kernel(x_ref, y_ref, g_ref, d_ref, wglu_ref, o_ref):
    g = g_ref[...]
    dsk = d_ref[...]
    for j in range(S5_CHUNK):
        xj = x_ref[:, j * D_MODEL:(j + 1) * D_MODEL]
        u = _rms(xj, g)
        y = jnp.concatenate([y_ref[s, :, j * LANES:(j + 1) * LANES] for s in range(N_SLABS)], axis=1)
        z = jax.nn.gelu(y + dsk * u, approximate=True).astype(BF16)
        zz = jnp.dot(z, wglu_ref[...], preferred_element_type=F32)
        za, zb = zz[:, :D_MODEL], zz[:, D_MODEL:]
        o_ref[:, j * D_MODEL:(j + 1) * D_MODEL] = xj + za * (1.0 / (1.0 + jnp.exp(-zb)))


def _s5_layer(x2, g_mix, cw1, cwout, pw, d_skip, w_glu):
    n_rows = x2.shape[0]
    nb = min(128, n_rows)
    nbc = min(256, n_rows)
    n_steps = int(math.log2(nbc))
    assert n_rows % nb == 0 and n_rows % nbc == 0 and (1 << n_steps) == nbc
    row_w = S5_CHUNK * D_MODEL

    u = pl.pallas_call(
        _s5_pre_kernel,
        grid=(n_rows // nb,),
        in_specs=[pl.BlockSpec((nb, row_w), lambda i: (i, 0)),
                  pl.BlockSpec((1, D_MODEL), lambda i: (0, 0))],
        out_specs=pl.BlockSpec((N_SLABS, nb, SLAB_W), lambda i: (0, i, 0)),
        out_shape=jax.ShapeDtypeStruct((N_SLABS, n_rows, SLAB_W), BF16),
        compiler_params=_cparams(("parallel",)),
        name="s5_pre",
    )(x2, g_mix)

    y = pl.pallas_call(
        functools.partial(_s5_core_kernel, n_steps=n_steps),
        grid=(N_SLABS, n_rows // nbc),
        in_specs=[pl.BlockSpec((1, nbc, SLAB_W), lambda s, r: (s, r, 0)),
                  pl.BlockSpec((1, SLAB_W, 2 * LANES), lambda s, r: (s, 0, 0)),
                  pl.BlockSpec((1, 2 * SLAB_STATE, LANES), lambda s, r: (s, 0, 0)),
                  pl.BlockSpec((2 * LANES, SLAB_W + 2 * SLAB_STATE), lambda s, r: (0, 0)),
                  pl.BlockSpec((1, n_steps, 2, SLAB_STATE), lambda s, r: (s, 0, 0, 0))],
        out_specs=pl.BlockSpec((1, nbc, SLAB_W), lambda s, r: (s, r, 0)),
        out_shape=jax.ShapeDtypeStruct((N_SLABS, n_rows, SLAB_W), F32),
        scratch_shapes=[pltpu.VMEM((1, 2 * SLAB_STATE), F32),
                        pltpu.VMEM((SLAB_W, SLAB_W + 2 * SLAB_STATE), BF16),
                        pltpu.VMEM((2 * SLAB_STATE, SLAB_W), BF16)],
        compiler_params=_cparams(("arbitrary", "arbitrary")),
        name="s5_core",
    )(u, cw1, cwout, _expansion_matrix(), pw)

    return pl.pallas_call(
        _s5_post_kernel,
        grid=(n_rows // nb,),
        in_specs=[pl.BlockSpec((nb, row_w), lambda i: (i, 0)),
                  pl.BlockSpec((N_SLABS, nb, SLAB_W), lambda i: (0, i, 0)),
                  pl.BlockSpec((1, D_MODEL), lambda i: (0, 0)),
                  pl.BlockSpec((1, D_MODEL), lambda i: (0, 0)),
                  pl.BlockSpec((D_MODEL, 2 * D_MODEL), lambda i: (0, 0))],
        out_specs=pl.BlockSpec((nb, row_w), lambda i: (i, 0)),
        out_shape=jax.ShapeDtypeStruct(x2.shape, F32),
        compiler_params=_cparams(("parallel",)),
        name="s5_post",
    )(x2, y, g_mix, d_skip, w_glu)


def _ffn_body(h, g, w1_ref, w3_ref, w2_ref):
    hn = _rms(h, g).astype(BF16)
    acc = jnp.zeros(h.shape, F32)
    for f in range(FFN_HIDDEN // FFN_TILE):
        sl = slice(f * FFN_TILE, (f + 1) * FFN_TILE)
        a = jnp.dot(hn, w1_ref[:, sl], preferred_element_type=F32)
        b = jnp.dot(hn, w3_ref[:, sl], preferred_element_type=F32)
        t = (a * (1.0 / (1.0 + jnp.exp(-a))) * b).astype(BF16)
        acc = acc + jnp.dot(t, w2_ref[sl, :], preferred_element_type=F32)
    return h + acc


def _ffn_kernel(h_ref, g_ref, w1_ref, w3_ref, w2_ref, o_ref):
    o_ref[...] = _ffn_body(h_ref[...], g_ref[...], w1_ref, w3_ref, w2_ref)


def _attn_out_ffn_kernel(h_ref, a_ref, wo_ref, g_ref, w1_ref, w3_ref, w2_ref, gf_ref, o_ref):
    h = h_ref[...] + jnp.dot(a_ref[...], wo_ref[...], preferred_element_type=F32)
    o_ref[...] = _rms(_ffn_body(h, g_ref[...], w1_ref, w3_ref, w2_ref), gf_ref[...])


def _const_spec(shape):
    return pl.BlockSpec(shape, lambda i: (0,) * len(shape), pipeline_mode=pl.Buffered(1))


def _ffn_layer(h, g, w1, w3, w2):
    n, tm = h.shape[0], min(512, h.shape[0])
    return pl.pallas_call(
        _ffn_kernel,
        grid=(n // tm,),
        in_specs=[pl.BlockSpec((tm, D_MODEL), lambda i: (i, 0)),
                  _const_spec((1, D_MODEL)),
                  _const_spec((D_MODEL, FFN_HIDDEN)),
                  _const_spec((D_MODEL, FFN_HIDDEN)),
                  _const_spec((FFN_HIDDEN, D_MODEL))],
        out_specs=pl.BlockSpec((tm, D_MODEL), lambda i: (i, 0)),
        out_shape=jax.ShapeDtypeStruct(h.shape, F32),
        compiler_params=_cparams(("parallel",)),
        name="ffn",
    )(h, g, w1, w3, w2)


def _attn_out_ffn_layer(h, a, wo, g, w1, w3, w2, gf):
    n, tm = h.shape[0], min(512, h.shape[0])
    return pl.pallas_call(
        _attn_out_ffn_kernel,
        grid=(n // tm,),
        in_specs=[pl.BlockSpec((tm, D_MODEL), lambda i: (i, 0)),
                  pl.BlockSpec((tm, D_MODEL), lambda i: (i, 0)),
                  _const_spec((D_MODEL, D_MODEL)),
                  _const_spec((1, D_MODEL)),
                  _const_spec((D_MODEL, FFN_HIDDEN)),
                  _const_spec((D_MODEL, FFN_HIDDEN)),
                  _const_spec((FFN_HIDDEN, D_MODEL)),
                  _const_spec((1, D_MODEL))],
        out_specs=pl.BlockSpec((tm, D_MODEL), lambda i: (i, 0)),
        out_shape=jax.ShapeDtypeStruct(h.shape, F32),
        compiler_params=_cparams(("parallel",)),
        name="attn_out_ffn",
    )(h, a, wo, g, w1, w3, w2, gf)


def _rope(t, cos, sin_lo, sin_hi):
    return t * cos + pltpu.roll(t, LANES - ROT_DIM // 2, 1) * sin_lo + pltpu.roll(t, ROT_DIM // 2, 1) * sin_hi


def _proj_kernel(h_ref, gkv_ref, gq_ref, wkv_ref, wq_ref, cos_ref, slo_ref, shi_ref, kk_ref, v_ref, qq_ref):
    h = h_ref[...]
    cos, slo, shi = cos_ref[...], slo_ref[...], shi_ref[...]
    kv = jnp.dot(_rms(h, gkv_ref[...]).astype(BF16), wkv_ref[...], preferred_element_type=F32)
    q = jnp.dot(_rms(h, gq_ref[...]).astype(BF16), wq_ref[...], preferred_element_type=F32)
    for hd in range(N_HEADS):
        sl = slice(hd * LANES, (hd + 1) * LANES)
        kk_ref[hd] = _rope(kv[:, sl], cos, slo, shi).astype(BF16)
        qq_ref[hd] = (_rope(q[:, sl], cos, slo, shi) * Q_SCALE).astype(BF16)
        v_ref[hd] = kv[:, D_MODEL + hd * LANES:D_MODEL + (hd + 1) * LANES].astype(BF16)


def _proj_layer(h, gkv, gq, wkv, wq, cos, slo, shi):
    n, tm = h.shape[0], min(512, h.shape[0])
    hshape = jax.ShapeDtypeStruct((N_HEADS, n, LANES), BF16)
    hspec = pl.BlockSpec((N_HEADS, tm, LANES), lambda i: (0, i, 0))
    tspec = pl.BlockSpec((tm, LANES), lambda i: (i, 0))
    return pl.pallas_call(
        _proj_kernel,
        grid=(n // tm,),
        in_specs=[pl.BlockSpec((tm, D_MODEL), lambda i: (i, 0)),
                  _const_spec((1, D_MODEL)), _const_spec((1, D_MODEL)),
                  _const_spec((D_MODEL, 2 * D_MODEL)), _const_spec((D_MODEL, D_MODEL)),
                  tspec, tspec, tspec],
        out_specs=[hspec, hspec, hspec],
        out_shape=[hshape, hshape, hshape],
        compiler_params=_cparams(("parallel",)),
        name="qkv_proj",
    )(h, gkv, gq, wkv, wq, cos, slo, shi)


def _rope_tables(n):
    half = ROT_DIM // 2
    inv_freq = ROPE_THETA ** (-jnp.arange(half, dtype=F32) * 2.0 / ROT_DIM)
    ang = jnp.arange(n, dtype=jnp.int32).astype(F32)[:, None] * inv_freq[None, :]
    cos, sin = jnp.cos(ang), jnp.sin(ang)
    ones = jnp.ones((n, HEAD_DIM - ROT_DIM), F32)
    zeros = jnp.zeros((n, HEAD_DIM - half), F32)
    cos_h = jnp.concatenate([cos, cos, ones], axis=1)
    lo_h = jnp.concatenate([-sin, zeros], axis=1)
    hi_h = jnp.concatenate([jnp.zeros((n, half), F32), sin, jnp.zeros((n, HEAD_DIM - ROT_DIM), F32)], axis=1)
    two = lambda t: jnp.concatenate([t, t], axis=1)
    return two(cos_h), two(lo_h), two(hi_h)


def _attn_kernel(lam_ref, qq_ref, kk_ref, v_ref, g_ref, o_ref, q2_ref, sa_ref, sb_ref, ma_ref, mb_ref,
                 ala_ref, alb_ref, mrun_ref, acco_ref, accl_ref, *, tq, tk, lam_init):
    qi = pl.program_id(1)
    q = qq_ref[0]
    lane = lax.broadcasted_iota(jnp.int32, q.shape, 1)
    zero = jnp.zeros_like(q)
    q2_ref[:tq, :] = jnp.where(lane < HEAD_DIM, q, zero)
    q2_ref[tq:, :] = jnp.where(lane >= HEAD_DIM, q, zero)
    mrun_ref[...] = jnp.full(mrun_ref.shape, -jnp.inf, F32)
    acco_ref[...] = jnp.zeros(acco_ref.shape, F32)
    accl_ref[...] = jnp.zeros(accl_ref.shape, F32)
    buf_a = (sa_ref, ma_ref, ala_ref)
    buf_b = (sb_ref, mb_ref, alb_ref)

    def scores(blk, buf, mask_off):
        s_ref, m_ref, al_ref = buf
        k = kk_ref[0, pl.ds(pl.multiple_of(blk * tk, tk), tk), :]
        s = lax.dot_general(q2_ref[...], k, (((1,), (1,)), ((), ())), preferred_element_type=F32)
        if mask_off is not None:
            r = lax.broadcasted_iota(jnp.int32, s.shape, 0)
            c = lax.broadcasted_iota(jnp.int32, s.shape, 1)
            r = jnp.where(r >= tq, r - tq, r)
            s = jnp.where(((c + mask_off) // CHUNK) <= (r // CHUNK), s, -jnp.inf)
        s_ref[...] = s
        m_prev = mrun_ref[...]
        m_new = jnp.maximum(m_prev, jnp.max(s, axis=1, keepdims=True))
        al_ref[...] = jnp.exp2(m_prev - m_new)
        m_ref[...] = m_new
        mrun_ref[...] = m_new

    def accumulate(blk, buf):
        s_ref, m_ref, al_ref = buf
        v = v_ref[0, pl.ds(pl.multiple_of(blk * tk, tk), tk), :]
        v1 = jnp.concatenate([v, jnp.ones_like(v)], axis=1)
        m = m_ref[...]
        p = jnp.concatenate([jnp.exp2(s_ref[:, j * LANES:(j + 1) * LANES] - m) for j in range(tk // LANES)],
                            axis=1).astype(BF16)
        pv = jnp.dot(p, v1, preferred_element_type=F32)
        al = al_ref[...]
        acco_ref[...] = al * acco_ref[...] + pv[:, :LANES]
        accl_ref[...] = al * accl_ref[...] + pv[:, LANES:]

    @pl.when(qi == 0)
    def _():
        scores(0, buf_a, 0)

    @pl.when(qi > 0)
    def _():
        scores(0, buf_a, None)

    def pair(t, next_mask):
        scores(2 * t + 1, buf_b, None)
        accumulate(2 * t, buf_a)
        scores(2 * t + 2, buf_a, next_mask)
        accumulate(2 * t + 1, buf_b)

    def body(t, c):
        pair(t, None)
        return c

    lax.fori_loop(0, qi - 1, body, 0)

    @pl.when(qi > 0)
    def _():
        pair(qi - 1, 0)

    scores(2 * qi + 1, buf_b, tk)
    accumulate(2 * qi, buf_a)
    accumulate(2 * qi + 1, buf_b)

    o = acco_ref[...] / accl_ref[...]
    o = o[:tq] - lam_ref[0] * o[tq:]
    o_ref[...] = (_rms(o, g_ref[...]) * (1.0 - lam_init)).astype(o_ref.dtype)


def _attn_layer(lam, qq, kk, v, g, lam_init):
    n = qq.shape[1]
    tq = min(1024, n)
    tk = tq // 2
    assert n % tq == 0 and tk % CHUNK == 0 and tk % LANES == 0
    stat = pltpu.VMEM((2 * tq, LANES), F32)
    return pl.pallas_call(
        functools.partial(_attn_kernel, tq=tq, tk=tk, lam_init=lam_init),
        grid=(N_HEADS, n // tq),
        in_specs=[pl.BlockSpec(memory_space=pltpu.SMEM),
                  pl.BlockSpec((1, tq, LANES), lambda h, i: (h, i, 0)),
                  pl.BlockSpec((1, n, LANES), lambda h, i: (h, 0, 0)),
                  pl.BlockSpec((1, n, LANES), lambda h, i: (h, 0, 0)),
                  pl.BlockSpec((1, LANES), lambda h, i: (0, 0))],
        out_specs=pl.BlockSpec((tq, LANES), lambda h, i: (i, h)),
        out_shape=jax.ShapeDtypeStruct((n, N_HEADS * V_DIM), BF16),
        scratch_shapes=[pltpu.VMEM((2 * tq, LANES), BF16),
                        pltpu.VMEM((2 * tq, tk), F32), pltpu.VMEM((2 * tq, tk), F32),
                        stat, stat, stat, stat, stat, stat, stat],
        compiler_params=_cparams(("parallel", "arbitrary")),
        name="diff_attn",
    )(lam, qq, kk, v, g)


def _head_pairs(w):
    return w.reshape(D_MODEL, 2, N_HEADS, HEAD_DIM).transpose(0, 2, 1, 3).reshape(D_MODEL, 2 * N_HEADS * HEAD_DIM)


def kernel(x, norm_mix_g, norm_ffn_g, ffn_w1, ffn_w3, ffn_w2, ssm_lam_re, ssm_lam_im, ssm_log_dt, ssm_b_re, ssm_b_im, ssm_c_re, ssm_c_im, ssm_d, ssm_w_glu, kv_norm_g, w_kv, attn_w_q, attn_lq1, attn_lk1, attn_lq2, attn_lk2, attn_subln_g, attn_w_o, final_norm_g):
    bsz, L, _ = x.shape
    assert bsz == 1 and L % (S5_CHUNK * 8) == 0
    row = lambda t: t.reshape(1, -1).astype(F32)

    cw1, cwout, a_chunk = _s5_weights(ssm_lam_re[0], ssm_lam_im[0], ssm_log_dt[0], ssm_b_re[0], ssm_b_im[0],
                                      ssm_c_re[0], ssm_c_im[0])
    n_rows = L // S5_CHUNK
    pw = _scan_powers(a_chunk, int(math.log2(min(256, n_rows))))
    x2 = x.reshape(n_rows, S5_CHUNK * D_MODEL)
    h = _s5_layer(x2, row(norm_mix_g[0]), cw1, cwout, pw, row(ssm_d[0]), ssm_w_glu[0].astype(BF16))
    h = h.reshape(L, D_MODEL)
    h = _ffn_layer(h, row(norm_ffn_g[0]), ffn_w1[0].astype(BF16), ffn_w3[0].astype(BF16), ffn_w2[0].astype(BF16))

    hk = N_HEADS * HEAD_DIM
    wkv = jnp.concatenate([_head_pairs(w_kv[:, :2 * hk]), w_kv[:, 2 * hk:]], axis=1).astype(BF16)
    wq = _head_pairs(attn_w_q[0]).astype(BF16)
    cos, slo, shi = _rope_tables(L)
    kk, v, qq = _proj_layer(h, row(kv_norm_g), row(norm_mix_g[1]), wkv, wq, cos, slo, shi)
    lam_init = 0.8 - 0.6 * math.exp(-0.3 * N_A_LAYERS)
    lam = (jnp.exp(jnp.sum(attn_lq1[0].astype(F32) * attn_lk1[0].astype(F32)))
           - jnp.exp(jnp.sum(attn_lq2[0].astype(F32) * attn_lk2[0].astype(F32))) + lam_init).reshape(1)
    a = _attn_layer(lam, qq, kk, v, row(attn_subln_g[0]), lam_init)
    out = _attn_out_ffn_layer(h, a, attn_w_o[0].astype(BF16), row(norm_ffn_g[1]), ffn_w1[1].astype(BF16),
                              ffn_w3[1].astype(BF16), ffn_w2[1].astype(BF16), row(final_norm_g))
    return out.reshape(1, L, D_MODEL)
```

```python
import functools
import math

import jax
import jax.numpy as jnp
import numpy as np
from jax import lax
from jax.experimental import pallas as pl
from jax.experimental.pallas import tpu as pltpu

F32 = jnp.float32
BF16 = jnp.bfloat16

D_MODEL = 1024
CHUNK = 64
SSM_GROUP = 16
N_GROUPS = D_MODEL // SSM_GROUP
SSM_STATE = 64
N_HEADS = 8
HEAD_DIM = 64
V_DIM = 2 * HEAD_DIM
ROT_DIM = HEAD_DIM // 4
ROPE_THETA = 500000.0
FFN_HIDDEN = 2816
EPS = 1e-6
N_A_LAYERS = 1

LANES = 128
S5_CHUNK = 8
N_SLABS = D_MODEL // LANES
GROUPS_PER_SLAB = LANES // SSM_GROUP
SLAB_STATE = GROUPS_PER_SLAB * SSM_STATE
SLAB_W = S5_CHUNK * LANES
FFN_TILE = 256
Q_SCALE = HEAD_DIM ** -0.5 * math.log2(math.e)
VMEM_LIMIT = 56 * 1024 * 1024


def _cparams(sem):
    return pltpu.CompilerParams(dimension_semantics=sem, vmem_limit_bytes=VMEM_LIMIT)


def _rms(x, g):
    return x * lax.rsqrt(jnp.mean(x * x, axis=-1, keepdims=True) + EPS) * g


def _s5_weights(lam_re, lam_im, log_dt, b_re, b_im, c_re, c_im):
    hi = lax.Precision.HIGHEST
    lr, li = lam_re.astype(F32), lam_im.astype(F32)
    dt = jnp.exp(log_dt.astype(F32))[:, None]
    mag = jnp.exp(lr * dt)
    ar, ai = mag * jnp.cos(li * dt), mag * jnp.sin(li * dt)
    nr, ni = ar - 1.0, ai
    den = lr * lr + li * li
    fr = (nr * lr + ni * li) / den
    fi = (ni * lr - nr * li) / den
    br, bi = b_re.astype(F32), b_im.astype(F32)
    bbr = fr[..., None] * br - fi[..., None] * bi
    bbi = fr[..., None] * bi + fi[..., None] * br
    cr, ci = c_re.astype(F32), c_im.astype(F32)

    prs, pis = [jnp.ones_like(ar)], [jnp.zeros_like(ar)]
    for _ in range(S5_CHUNK):
        prs.append(prs[-1] * ar - pis[-1] * ai)
        pis.append(prs[-2] * ai + pis[-1] * ar)
    pr = jnp.stack(prs)
    pi = jnp.stack(pis)

    T = S5_CHUNK
    kr = pr[:T, :, :, None] * bbr[None] - pi[:T, :, :, None] * bbi[None]
    ki = pr[:T, :, :, None] * bbi[None] + pi[:T, :, :, None] * bbr[None]
    klag = (jnp.einsum('gdp,kgpc->kgcd', cr, kr, precision=hi)
            - jnp.einsum('gdp,kgpc->kgcd', ci, ki, precision=hi))
    tt = jnp.arange(T)
    lag = tt[None, :] - tt[:, None]
    toe = jnp.where((lag >= 0)[:, :, None, None, None],
                    klag[jnp.clip(lag, 0, T - 1)], 0.0)
    win_r = kr[::-1]
    win_i = ki[::-1]
    p1r, p1i = pr[1:], pi[1:]
    out_r = cr[None] * p1r[:, :, None, :] - ci[None] * p1i[:, :, None, :]
    out_i = -cr[None] * p1i[:, :, None, :] - ci[None] * p1r[:, :, None, :]

    S, Gs = N_SLABS, GROUPS_PER_SLAB
    c_toe = toe.reshape(T, T, S, Gs, SSM_GROUP, SSM_GROUP).transpose(2, 0, 3, 4, 1, 5).reshape(S, SLAB_W, LANES)
    to_rows = lambda w: w.reshape(T, S, Gs, SSM_STATE, SSM_GROUP).transpose(1, 0, 2, 4, 3).reshape(S, SLAB_W, SSM_STATE)
    cw1 = jnp.concatenate([c_toe, to_rows(win_r), to_rows(win_i)], axis=-1).astype(BF16)
    to_out = lambda w: w.reshape(T, S, Gs, SSM_GROUP, SSM_STATE).transpose(1, 2, 4, 0, 3).reshape(S, SLAB_STATE, LANES)
    cwout = jnp.concatenate([to_out(out_r), to_out(out_i)], axis=1).astype(BF16)
    a_chunk = (pr[T].reshape(S, SLAB_STATE), pi[T].reshape(S, SLAB_STATE))
    return cw1, cwout, a_chunk


def _expansion_matrix():
    e = np.zeros((2 * LANES, SLAB_W + 2 * SLAB_STATE), np.float32)
    for h in range(GROUPS_PER_SLAB):
        for t in range(S5_CHUNK):
            for d in range(SSM_GROUP):
                e[t * SSM_GROUP + d, t * LANES + h * SSM_GROUP + d] = 1.0
        for p in range(SSM_STATE):
            e[LANES + p, SLAB_W + h * SSM_STATE + p] = 1.0
            e[LANES + SSM_STATE + p, SLAB_W + SLAB_STATE + h * SSM_STATE + p] = 1.0
    return jnp.asarray(e, BF16)


def _scan_powers(a_chunk, n_steps):
    r, i = a_chunk
    rows = []
    for _ in range(n_steps):
        rows.append(jnp.stack([r, i], axis=1))
        r, i = r * r - i * i, 2.0 * r * i
    return jnp.stack(rows, axis=1)


def _s5_pre_kernel(x_ref, g_ref, u_ref):
    g = g_ref[...]
    for j in range(S5_CHUNK):
        u = _rms(x_ref[:, j * D_MODEL:(j + 1) * D_MODEL], g).astype(BF16)
        for s in range(N_SLABS):
            u_ref[s, :, j * LANES:(j + 1) * LANES] = u[:, s * LANES:(s + 1) * LANES]


def _shift_rows(x, d, row):
    n = x.shape[0]
    if d % 8 == 0:
        return jnp.concatenate([jnp.zeros((d, x.shape[1]), x.dtype), x[:n - d]], axis=0)
    return jnp.where(row >= d, pltpu.roll(x, d, 0), 0.0)


def _expand_block_diag(c, e_ref, w_ref, row_period, row_group):
    n_rows = c.shape[0]
    tile = SLAB_STATE
    for ct in range(w_ref.shape[1] // tile):
        x = jnp.dot(c, e_ref[:c.shape[1], ct * tile:(ct + 1) * tile], preferred_element_type=F32)
        row = lax.broadcasted_iota(jnp.int32, (n_rows, tile), 0)
        col = lax.broadcasted_iota(jnp.int32, (n_rows, tile), 1)
        if ct * tile < SLAB_W:
            col_g = (col % LANES) // SSM_GROUP
        else:
            col_g = col // SSM_STATE
        keep = (row % row_period) // row_group == col_g
        w_ref[:, ct * tile:(ct + 1) * tile] = jnp.where(keep, x, 0.0).astype(w_ref.dtype)


def _s5_core_kernel(u_ref, cw1_ref, cwout_ref, e_ref, pw_ref, y_ref, carry_ref, w1_ref, wout_ref, *, n_steps):
    @pl.when(pl.program_id(1) == 0)
    def _():
        carry_ref[...] = jnp.zeros_like(carry_ref)
        _expand_block_diag(cw1_ref[0], e_ref, w1_ref, LANES, SSM_GROUP)
        _expand_block_diag(cwout_ref[0], e_ref, wout_ref, SLAB_STATE, SSM_STATE)

    res = jnp.dot(u_ref[0], w1_ref[...], preferred_element_type=F32)
    nb = res.shape[0]
    y_intra = res[:, :SLAB_W]
    er = res[:, SLAB_W:SLAB_W + SLAB_STATE]
    ei = res[:, SLAB_W + SLAB_STATE:]
    row = lax.broadcasted_iota(jnp.int32, (nb, SLAB_STATE), 0)
    c_r = carry_ref[:, :SLAB_STATE]
    c_i = carry_ref[:, SLAB_STATE:]
    a_r, a_i = pw_ref[0, 0, 0:1, :], pw_ref[0, 0, 1:2, :]
    first = row == 0
    er = er + jnp.where(first, a_r * c_r - a_i * c_i, 0.0)
    ei = ei + jnp.where(first, a_r * c_i + a_i * c_r, 0.0)
    for i in range(n_steps):
        d = 1 << i
        p_r, p_i = pw_ref[0, i, 0:1, :], pw_ref[0, i, 1:2, :]
        sr, si = _shift_rows(er, d, row), _shift_rows(ei, d, row)
        er, ei = er + p_r * sr - p_i * si, ei + p_r * si + p_i * sr
    prev_r = jnp.where(first, c_r, pltpu.roll(er, 1, 0))
    prev_i = jnp.where(first, c_i, pltpu.roll(ei, 1, 0))
    carry_ref[:, :SLAB_STATE] = er[nb - 1:nb, :]
    carry_ref[:, SLAB_STATE:] = ei[nb - 1:nb, :]
    prev = jnp.concatenate([prev_r, prev_i], axis=1).astype(BF16)
    y_ref[0] = y_intra + jnp.dot(prev, wout_ref[...], preferred_element_type=F32)


def _s5_post_kernel(x_ref, y_ref, g_ref, d_ref, wglu_ref, o_ref):
    g = g_ref[...]
    dsk = d_ref[...]
    for j in range(S5_CHUNK):
        xj = x_ref[:, j * D_MODEL:(j + 1) * D_MODEL]
        u = _rms(xj, g)
        y = jnp.concatenate([y_ref[s, :, j * LANES:(j + 1) * LANES] for s in range(N_SLABS)], axis=1)
        z = jax.nn.gelu(y + dsk * u, approximate=True).astype(BF16)
        zz = jnp.dot(z, wglu_ref[...], preferred_element_type=F32)
        za, zb = zz[:, :D_MODEL], zz[:, D_MODEL:]
        o_ref[:, j * D_MODEL:(j + 1) * D_MODEL] = xj + za * (1.0 / (1.0 + jnp.exp(-zb)))


def _s5_layer(x2, g_mix, cw1, cwout, pw, d_skip, w_glu):
    n_rows = x2.shape[0]
    nb = min(128, n_rows)
    nbc = min(256, n_rows)
    n_steps = int(math.log2(nbc))
    assert n_rows % nb == 0 and n_rows % nbc == 0 and (1 << n_steps) == nbc
    row_w = S5_CHUNK * D_MODEL

    u = pl.pallas_call(
        _s5_pre_kernel,
        grid=(n_rows // nb,),
        in_specs=[pl.BlockSpec((nb, row_w), lambda i: (i, 0)),
                  pl.BlockSpec((1, D_MODEL), lambda i: (0, 0))],
        out_specs=pl.BlockSpec((N_SLABS, nb, SLAB_W), lambda i: (0, i, 0)),
        out_shape=jax.ShapeDtypeStruct((N_SLABS, n_rows, SLAB_W), BF16),
        compiler_params=_cparams(("parallel",)),
        name="s5_pre",
    )(x2, g_mix)

    y = pl.pallas_call(
        functools.partial(_s5_core_kernel, n_steps=n_steps),
        grid=(N_SLABS, n_rows // nbc),
        in_specs=[pl.BlockSpec((1, nbc, SLAB_W), lambda s, r: (s, r, 0)),
                  pl.BlockSpec((1, SLAB_W, 2 * LANES), lambda s, r: (s, 0, 0)),
                  pl.BlockSpec((1, 2 * SLAB_STATE, LANES), lambda s, r: (s, 0, 0)),
                  pl.BlockSpec((2 * LANES, SLAB_W + 2 * SLAB_STATE), lambda s, r: (0, 0)),
                  pl.BlockSpec((1, n_steps, 2, SLAB_STATE), lambda s, r: (s, 0, 0, 0))],
        out_specs=pl.BlockSpec((1, nbc, SLAB_W), lambda s, r: (s, r, 0)),
        out_shape=jax.ShapeDtypeStruct((N_SLABS, n_rows, SLAB_W), F32),
        scratch_shapes=[pltpu.VMEM((1, 2 * SLAB_STATE), F32),
                        pltpu.VMEM((SLAB_W, SLAB_W + 2 * SLAB_STATE), BF16),
                        pltpu.VMEM((2 * SLAB_STATE, SLAB_W), BF16)],
        compiler_params=_cparams(("arbitrary", "arbitrary")),
        name="s5_core",
    )(u, cw1, cwout, _expansion_matrix(), pw)

    return pl.pallas_call(
        _s5_post_kernel,
        grid=(n_rows // nb,),
        in_specs=[pl.BlockSpec((nb, row_w), lambda i: (i, 0)),
                  pl.BlockSpec((N_SLABS, nb, SLAB_W), lambda i: (0, i, 0)),
                  pl.BlockSpec((1, D_MODEL), lambda i: (0, 0)),
                  pl.BlockSpec((1, D_MODEL), lambda i: (0, 0)),
                  pl.BlockSpec((D_MODEL, 2 * D_MODEL), lambda i: (0, 0))],
        out_specs=pl.BlockSpec((nb, row_w), lambda i: (i, 0)),
        out_shape=jax.ShapeDtypeStruct(x2.shape, F32),
        compiler_params=_cparams(("parallel",)),
        name="s5_post",
    )(x2, y, g_mix, d_skip, w_glu)


def _ffn_body(h, g, w1_ref, w3_ref, w2_ref):
    hn = _rms(h, g).astype(BF16)
    acc = jnp.zeros(h.shape, F32)
    for f in range(FFN_HIDDEN // FFN_TILE):
        sl = slice(f * FFN_TILE, (f + 1) * FFN_TILE)
        a = jnp.dot(hn, w1_ref[:, sl], preferred_element_type=F32)
        b = jnp.dot(hn, w3_ref[:, sl], preferred_element_type=F32)
        t = (a * (1.0 / (1.0 + jnp.exp(-a))) * b).astype(BF16)
        acc = acc + jnp.dot(t, w2_ref[sl, :], preferred_element_type=F32)
    return h + acc


def _ffn_kernel(h_ref, g_ref, w1_ref, w3_ref, w2_ref, o_ref):
    o_ref[...] = _ffn_body(h_ref[...], g_ref[...], w1_ref, w3_ref, w2_ref)


def _attn_out_ffn_kernel(h_ref, a_ref, wo_ref, g_ref, w1_ref, w3_ref, w2_ref, gf_ref, o_ref):
    h = h_ref[...] + jnp.dot(a_ref[...], wo_ref[...], preferred_element_type=F32)
    o_ref[...] = _rms(_ffn_body(h, g_ref[...], w1_ref, w3_ref, w2_ref), gf_ref[...])


def _const_spec(shape):
    return pl.BlockSpec(shape, lambda i: (0,) * len(shape), pipeline_mode=pl.Buffered(1))


def _ffn_layer(h, g, w1, w3, w2):
    n, tm = h.shape[0], min(512, h.shape[0])
    return pl.pallas_call(
        _ffn_kernel,
        grid=(n // tm,),
        in_specs=[pl.BlockSpec((tm, D_MODEL), lambda i: (i, 0)),
                  _const_spec((1, D_MODEL)),
                  _const_spec((D_MODEL, FFN_HIDDEN)),
                  _const_spec((D_MODEL, FFN_HIDDEN)),
                  _const_spec((FFN_HIDDEN, D_MODEL))],
        out_specs=pl.BlockSpec((tm, D_MODEL), lambda i: (i, 0)),
        out_shape=jax.ShapeDtypeStruct(h.shape, F32),
        compiler_params=_cparams(("parallel",)),
        name="ffn",
    )(h, g, w1, w3, w2)


def _attn_out_ffn_layer(h, a, wo, g, w1, w3, w2, gf):
    n, tm = h.shape[0], min(512, h.shape[0])
    return pl.pallas_call(
        _attn_out_ffn_kernel,
        grid=(n // tm,),
        in_specs=[pl.BlockSpec((tm, D_MODEL), lambda i: (i, 0)),
                  pl.BlockSpec((tm, D_MODEL), lambda i: (i, 0)),
                  _const_spec((D_MODEL, D_MODEL)),
                  _const_spec((1, D_MODEL)),
                  _const_spec((D_MODEL, FFN_HIDDEN)),
                  _const_spec((D_MODEL, FFN_HIDDEN)),
                  _const_spec((FFN_HIDDEN, D_MODEL)),
                  _const_spec((1, D_MODEL))],
        out_specs=pl.BlockSpec((tm, D_MODEL), lambda i: (i, 0)),
        out_shape=jax.ShapeDtypeStruct(h.shape, F32),
        compiler_params=_cparams(("parallel",)),
        name="attn_out_ffn",
    )(h, a, wo, g, w1, w3, w2, gf)


def _rope(t, cos, sin_lo, sin_hi):
    return t * cos + pltpu.roll(t, LANES - ROT_DIM // 2, 1) * sin_lo + pltpu.roll(t, ROT_DIM // 2, 1) * sin_hi


def _pair_heads(x, hd, lane):
    hk = N_HEADS * HEAD_DIM
    c0 = (hd // 2) * LANES
    t1, t2 = x[:, c0:c0 + LANES], x[:, hk + c0:hk + c0 + LANES]
    if hd % 2 == 0:
        return jnp.where(lane < HEAD_DIM, t1, pltpu.roll(t2, HEAD_DIM, 1))
    return jnp.where(lane < HEAD_DIM, pltpu.roll(t1, HEAD_DIM, 1), t2)


def _proj_kernel(h_ref, gkv_ref, gq_ref, wkv_ref, wq_ref, cs_ref, sel_ref, one_ref, kk_ref, v_ref, qq_ref):
    h = h_ref[...]
    tab = jnp.dot(cs_ref[...], sel_ref[...], preferred_element_type=F32)
    cos, slo, shi = tab[:, :LANES] + one_ref[...], tab[:, LANES:2 * LANES], tab[:, 2 * LANES:]
    kv = jnp.dot(_rms(h, gkv_ref[...]).astype(BF16), wkv_ref[...], preferred_element_type=F32)
    q = jnp.dot(_rms(h, gq_ref[...]).astype(BF16), wq_ref[...], preferred_element_type=F32)
    lane = lax.broadcasted_iota(jnp.int32, (h.shape[0], LANES), 1)
    for hd in range(N_HEADS):
        kk_ref[hd] = _rope(_pair_heads(kv, hd, lane), cos, slo, shi).astype(BF16)
        qq_ref[hd] = (_rope(_pair_heads(q, hd, lane), cos, slo, shi) * Q_SCALE).astype(BF16)
        v_ref[hd] = kv[:, D_MODEL + hd * LANES:D_MODEL + (hd + 1) * LANES].astype(BF16)


def _proj_layer(h, gkv, gq, wkv, wq, cs, sel, one):
    n, tm = h.shape[0], min(512, h.shape[0])
    hshape = jax.ShapeDtypeStruct((N_HEADS, n, LANES), BF16)
    hspec = pl.BlockSpec((N_HEADS, tm, LANES), lambda i: (0, i, 0))
    return pl.pallas_call(
        _proj_kernel,
        grid=(n // tm,),
        in_specs=[pl.BlockSpec((tm, D_MODEL), lambda i: (i, 0)),
                  _const_spec((1, D_MODEL)), _const_spec((1, D_MODEL)),
                  _const_spec((D_MODEL, 2 * D_MODEL)), _const_spec((D_MODEL, D_MODEL)),
                  pl.BlockSpec((tm, cs.shape[1]), lambda i: (i, 0)),
                  _const_spec(sel.shape), _const_spec((1, LANES))],
        out_specs=[hspec, hspec, hspec],
        out_shape=[hshape, hshape, hshape],
        compiler_params=_cparams(("parallel",)),
        name="qkv_proj",
    )(h, gkv, gq, wkv, wq, cs, sel, one)


def _rope_tables(n):
    half = ROT_DIM // 2
    inv_freq = ROPE_THETA ** (-jnp.arange(half, dtype=F32) * 2.0 / ROT_DIM)
    ang = inv_freq[:, None] * jnp.arange(n, dtype=jnp.int32).astype(F32)[None, :]
    cs = jnp.concatenate([jnp.cos(ang), jnp.sin(ang)], axis=0).T
    hi = cs.astype(BF16)
    lo = (cs - hi.astype(F32)).astype(BF16)
    sel = np.zeros((ROT_DIM, 3 * LANES), np.float32)
    one = np.ones((1, LANES), np.float32)
    for l in range(LANES):
        d = l % HEAD_DIM
        if d < ROT_DIM:
            one[0, l] = 0.0
            sel[d % half, l] = 1.0
            if d < half:
                sel[half + d, LANES + l] = -1.0
            else:
                sel[half + d - half, 2 * LANES + l] = 1.0
    sel = np.concatenate([sel, sel], axis=0)
    return jnp.concatenate([hi, lo], axis=1), jnp.asarray(sel, BF16), jnp.asarray(one)


def _attn_kernel(lam_ref, qq_ref, kk_ref, v_ref, g_ref, o_ref, q2_ref, sa_ref, sb_ref, ma_ref, mb_ref,
                 ala_ref, alb_ref, mrun_ref, acco_ref, accl_ref, *, tq, tk, lam_init):
    qi = pl.program_id(1)
    q = qq_ref[0]
    lane = lax.broadcasted_iota(jnp.int32, q.shape, 1)
    zero = jnp.zeros_like(q)
    q2_ref[:tq, :] = jnp.where(lane < HEAD_DIM, q, zero)
    q2_ref[tq:, :] = jnp.where(lane >= HEAD_DIM, q, zero)
    mrun_ref[...] = jnp.full(mrun_ref.shape, -jnp.inf, F32)
    acco_ref[...] = jnp.zeros(acco_ref.shape, F32)
    accl_ref[...] = jnp.zeros(accl_ref.shape, F32)
    buf_a = (sa_ref, ma_ref, ala_ref)
    buf_b = (sb_ref, mb_ref, alb_ref)

    def scores(blk, buf, mask_off):
        s_ref, m_ref, al_ref = buf
        k = kk_ref[0, pl.ds(pl.multiple_of(blk * tk, tk), tk), :]
        s = lax.dot_general(q2_ref[...], k, (((1,), (1,)), ((), ())), preferred_element_type=F32)
        if mask_off is not None:
            r = lax.broadcasted_iota(jnp.int32, s.shape, 0)
            c = lax.broadcasted_iota(jnp.int32, s.shape, 1)
            r = jnp.where(r >= tq, r - tq, r)
            s = jnp.where(((c + mask_off) // CHUNK) <= (r // CHUNK), s, -jnp.inf)
        s_ref[...] = s
        m_prev = mrun_ref[...]
        m_new = jnp.maximum(m_prev, jnp.max(s, axis=1, keepdims=True))
        al_ref[...] = jnp.exp2(m_prev - m_new)
        m_ref[...] = m_new
        mrun_ref[...] = m_new

    def accumulate(blk, buf):
        s_ref, m_ref, al_ref = buf
        v = v_ref[0, pl.ds(pl.multiple_of(blk * tk, tk), tk), :]
        v1 = jnp.concatenate([v, jnp.ones_like(v)], axis=1)
        m = m_ref[...]
        p = jnp.concatenate([jnp.exp2(s_ref[:, j * LANES:(j + 1) * LANES] - m) for j in range(tk // LANES)],
                            axis=1).astype(BF16)
        pv = jnp.dot(p, v1, preferred_element_type=F32)
        al = al_ref[...]
        acco_ref[...] = al * acco_ref[...] + pv[:, :LANES]
        accl_ref[...] = al * accl_ref[...] + pv[:, LANES:]

    @pl.when(qi == 0)
    def _():
        scores(0, buf_a, 0)

    @pl.when(qi > 0)
    def _():
        scores(0, buf_a, None)

    def pair(t, next_mask):
        scores(2 * t + 1, buf_b, None)
        accumulate(2 * t, buf_a)
        scores(2 * t + 2, buf_a, next_mask)
        accumulate(2 * t + 1, buf_b)

    def body(t, c):
        pair(t, None)
        return c

    lax.fori_loop(0, qi - 1, body, 0)

    @pl.when(qi > 0)
    def _():
        pair(qi - 1, 0)

    scores(2 * qi + 1, buf_b, tk)
    accumulate(2 * qi, buf_a)
    accumulate(2 * qi + 1, buf_b)

    o = acco_ref[...] / accl_ref[...]
    o = o[:tq] - lam_ref[0] * o[tq:]
    o_ref[...] = (_rms(o, g_ref[...]) * (1.0 - lam_init)).astype(o_ref.dtype)


def _attn_layer(lam, qq, kk, v, g, lam_init):
    n = qq.shape[1]
    tq = min(1024, n)
    tk = tq // 2
    assert n % tq == 0 and tk % CHUNK == 0 and tk % LANES == 0
    stat = pltpu.VMEM((2 * tq, LANES), F32)
    return pl.pallas_call(
        functools.partial(_attn_kernel, tq=tq, tk=tk, lam_init=lam_init),
        grid=(N_HEADS, n // tq),
        in_specs=[pl.BlockSpec(memory_space=pltpu.SMEM),
                  pl.BlockSpec((1, tq, LANES), lambda h, i: (h, i, 0)),
                  pl.BlockSpec((1, n, LANES), lambda h, i: (h, 0, 0)),
                  pl.BlockSpec((1, n, LANES), lambda h, i: (h, 0, 0)),
                  pl.BlockSpec((1, LANES), lambda h, i: (0, 0))],
        out_specs=pl.BlockSpec((tq, LANES), lambda h, i: (i, h)),
        out_shape=jax.ShapeDtypeStruct((n, N_HEADS * V_DIM), BF16),
        scratch_shapes=[pltpu.VMEM((2 * tq, LANES), BF16),
                        pltpu.VMEM((2 * tq, tk), F32), pltpu.VMEM((2 * tq, tk), F32),
                        stat, stat, stat, stat, stat, stat, stat],
        compiler_params=_cparams(("parallel", "arbitrary")),
        name="diff_attn",
    )(lam, qq, kk, v, g)


def kernel(x, norm_mix_g, norm_ffn_g, ffn_w1, ffn_w3, ffn_w2, ssm_lam_re, ssm_lam_im, ssm_log_dt, ssm_b_re, ssm_b_im, ssm_c_re, ssm_c_im, ssm_d, ssm_w_glu, kv_norm_g, w_kv, attn_w_q, attn_lq1, attn_lk1, attn_lq2, attn_lk2, attn_subln_g, attn_w_o, final_norm_g):
    bsz, L, _ = x.shape
    assert bsz == 1 and L % (S5_CHUNK * 8) == 0
    row = lambda t: t.reshape(1, -1).astype(F32)

    cw1, cwout, a_chunk = _s5_weights(ssm_lam_re[0], ssm_lam_im[0], ssm_log_dt[0], ssm_b_re[0], ssm_b_im[0],
                                      ssm_c_re[0], ssm_c_im[0])
    n_rows = L // S5_CHUNK
    pw = _scan_powers(a_chunk, int(math.log2(min(256, n_rows))))
    x2 = x.reshape(n_rows, S5_CHUNK * D_MODEL)
    h = _s5_layer(x2, row(norm_mix_g[0]), cw1, cwout, pw, row(ssm_d[0]), ssm_w_glu[0].astype(BF16))
    h = h.reshape(L, D_MODEL)
    h = _ffn_layer(h, row(norm_ffn_g[0]), ffn_w1[0].astype(BF16), ffn_w3[0].astype(BF16), ffn_w2[0].astype(BF16))

    hk = N_HEADS * HEAD_DIM
    cs, sel, one = _rope_tables(L)
    kk, v, qq = _proj_layer(h, row(kv_norm_g), row(norm_mix_g[1]), w_kv.astype(BF16), attn_w_q[0].astype(BF16),
                            cs, sel, one)
    lam_init = 0.8 - 0.6 * math.exp(-0.3 * N_A_LAYERS)
    lam = (jnp.exp(jnp.sum(attn_lq1[0].astype(F32) * attn_lk1[0].astype(F32)))
           - jnp.exp(jnp.sum(attn_lq2[0].astype(F32) * attn_lk2[0].astype(F32))) + lam_init).reshape(1)
    a = _attn_layer(lam, qq, kk, v, row(attn_subln_g[0]), lam_init)
    out = _attn_out_ffn_layer(h, a, attn_w_o[0].astype(BF16), row(norm_ffn_g[1]), ffn_w1[1].astype(BF16),
                              ffn_w3[1].astype(BF16), ffn_w2[1].astype(BF16), row(final_norm_g))
    return out.reshape(1, L, D_MODEL)
```

```python
import functools
import math

import jax
import jax.numpy as jnp
import numpy as np
from jax import lax
from jax.experimental import pallas as pl
from jax.experimental.pallas import tpu as pltpu

F32 = jnp.float32
BF16 = jnp.bfloat16

D_MODEL = 1024
CHUNK = 64
SSM_GROUP = 16
N_GROUPS = D_MODEL // SSM_GROUP
SSM_STATE = 64
N_HEADS = 8
HEAD_DIM = 64
V_DIM = 2 * HEAD_DIM
ROT_DIM = HEAD_DIM // 4
ROPE_THETA = 500000.0
FFN_HIDDEN = 2816
EPS = 1e-6
N_A_LAYERS = 1

LANES = 128
S5_CHUNK = 8
N_SLABS = D_MODEL // LANES
GROUPS_PER_SLAB = LANES // SSM_GROUP
SLAB_STATE = GROUPS_PER_SLAB * SSM_STATE
SLAB_W = S5_CHUNK * LANES
FFN_TILE = 256
Q_SCALE = HEAD_DIM ** -0.5 * math.log2(math.e)
VMEM_LIMIT = 56 * 1024 * 1024


def _cparams(sem):
    return pltpu.CompilerParams(dimension_semantics=sem, vmem_limit_bytes=VMEM_LIMIT)


def _rms(x, g):
    return x * lax.rsqrt(jnp.mean(x * x, axis=-1, keepdims=True) + EPS) * g


def _s5_weights(lam_re, lam_im, log_dt, b_re, b_im, c_re, c_im):
    hi = lax.Precision.HIGHEST
    lr, li = lam_re.astype(F32), lam_im.astype(F32)
    dt = jnp.exp(log_dt.astype(F32))[:, None]
    mag = jnp.exp(lr * dt)
    ar, ai = mag * jnp.cos(li * dt), mag * jnp.sin(li * dt)
    nr, ni = ar - 1.0, ai
    den = lr * lr + li * li
    fr = (nr * lr + ni * li) / den
    fi = (ni * lr - nr * li) / den
    br, bi = b_re.astype(F32), b_im.astype(F32)
    bbr = fr[..., None] * br - fi[..., None] * bi
    bbi = fr[..., None] * bi + fi[..., None] * br
    cr, ci = c_re.astype(F32), c_im.astype(F32)

    prs, pis = [jnp.ones_like(ar)], [jnp.zeros_like(ar)]
    for _ in range(S5_CHUNK):
        prs.append(prs[-1] * ar - pis[-1] * ai)
        pis.append(prs[-2] * ai + pis[-1] * ar)
    pr = jnp.stack(prs)
    pi = jnp.stack(pis)

    T = S5_CHUNK
    kr = pr[:T, :, :, None] * bbr[None] - pi[:T, :, :, None] * bbi[None]
    ki = pr[:T, :, :, None] * bbi[None] + pi[:T, :, :, None] * bbr[None]
    klag = (jnp.einsum('gdp,kgpc->kgcd', cr, kr, precision=hi)
            - jnp.einsum('gdp,kgpc->kgcd', ci, ki, precision=hi))
    tt = jnp.arange(T)
    lag = tt[None, :] - tt[:, None]
    toe = jnp.where((lag >= 0)[:, :, None, None, None],
                    klag[jnp.clip(lag, 0, T - 1)], 0.0)
    win_r = kr[::-1]
    win_i = ki[::-1]
    p1r, p1i = pr[1:], pi[1:]
    out_r = cr[None] * p1r[:, :, None, :] - ci[None] * p1i[:, :, None, :]
    out_i = -cr[None] * p1i[:, :, None, :] - ci[None] * p1r[:, :, None, :]

    S, Gs = N_SLABS, GROUPS_PER_SLAB
    c_toe = toe.reshape(T, T, S, Gs, SSM_GROUP, SSM_GROUP).transpose(2, 0, 3, 4, 1, 5).reshape(S, SLAB_W, LANES)
    to_rows = lambda w: w.reshape(T, S, Gs, SSM_STATE, SSM_GROUP).transpose(1, 0, 2, 4, 3).reshape(S, SLAB_W, SSM_STATE)
    cw1 = jnp.concatenate([c_toe, to_rows(win_r), to_rows(win_i)], axis=-1).astype(BF16)
    to_out = lambda w: w.reshape(T, S, Gs, SSM_GROUP, SSM_STATE).transpose(1, 2, 4, 0, 3).reshape(S, SLAB_STATE, LANES)
    cwout = jnp.concatenate([to_out(out_r), to_out(out_i)], axis=1).astype(BF16)
    a_chunk = (pr[T].reshape(S, SLAB_STATE), pi[T].reshape(S, SLAB_STATE))
    return cw1, cwout, a_chunk


def _expansion_matrix():
    e = np.zeros((2 * LANES, SLAB_W + 2 * SLAB_STATE), np.float32)
    for h in range(GROUPS_PER_SLAB):
        for t in range(S5_CHUNK):
            for d in range(SSM_GROUP):
                e[t * SSM_GROUP + d, t * LANES + h * SSM_GROUP + d] = 1.0
        for p in range(SSM_STATE):
            e[LANES + p, SLAB_W + h * SSM_STATE + p] = 1.0
            e[LANES + SSM_STATE + p, SLAB_W + SLAB_STATE + h * SSM_STATE + p] = 1.0
    return jnp.asarray(e, BF16)


def _scan_powers(a_chunk, n_steps):
    r, i = a_chunk
    rows = []
    for _ in range(n_steps):
        rows.append(jnp.stack([r, i], axis=1))
        r, i = r * r - i * i, 2.0 * r * i
    return jnp.stack(rows, axis=1)


def _s5_pre_kernel(x_ref, g_ref, u_ref):
    nb = u_ref.shape[1]
    u = _rms(x_ref[...], g_ref[...])
    uj = jnp.swapaxes(u.reshape(nb, S5_CHUNK, D_MODEL), 0, 1)
    for j in range(S5_CHUNK):
        for s in range(N_SLABS):
            u_ref[s, :, j * LANES:(j + 1) * LANES] = uj[j, :, s * LANES:(s + 1) * LANES].astype(BF16)


def _shift_rows(x, d, row):
    n = x.shape[0]
    if d % 8 == 0:
        return jnp.concatenate([jnp.zeros((d, x.shape[1]), x.dtype), x[:n - d]], axis=0)
    return jnp.where(row >= d, pltpu.roll(x, d, 0), 0.0)


def _expand_block_diag(c, e_ref, w_ref, row_period, row_group):
    n_rows = c.shape[0]
    tile = SLAB_STATE
    for ct in range(w_ref.shape[1] // tile):
        x = jnp.dot(c, e_ref[:c.shape[1], ct * tile:(ct + 1) * tile], preferred_element_type=F32)
        row = lax.broadcasted_iota(jnp.int32, (n_rows, tile), 0)
        col = lax.broadcasted_iota(jnp.int32, (n_rows, tile), 1)
        if ct * tile < SLAB_W:
            col_g = (col % LANES) // SSM_GROUP
        else:
            col_g = col // SSM_STATE
        keep = (row % row_period) // row_group == col_g
        w_ref[:, ct * tile:(ct + 1) * tile] = jnp.where(keep, x, 0.0).astype(w_ref.dtype)


def _s5_core_kernel(u_ref, cw1_ref, cwout_ref, e_ref, pw_ref, y_ref, carry_ref, w1_ref, wout_ref, *, n_steps):
    @pl.when(pl.program_id(1) == 0)
    def _():
        carry_ref[...] = jnp.zeros_like(carry_ref)
        _expand_block_diag(cw1_ref[0], e_ref, w1_ref, LANES, SSM_GROUP)
        _expand_block_diag(cwout_ref[0], e_ref, wout_ref, SLAB_STATE, SSM_STATE)

    res = jnp.dot(u_ref[0], w1_ref[...], preferred_element_type=F32)
    nb = res.shape[0]
    y_intra = res[:, :SLAB_W]
    er = res[:, SLAB_W:SLAB_W + SLAB_STATE]
    ei = res[:, SLAB_W + SLAB_STATE:]
    row = lax.broadcasted_iota(jnp.int32, (nb, SLAB_STATE), 0)
    c_r = carry_ref[:, :SLAB_STATE]
    c_i = carry_ref[:, SLAB_STATE:]
    a_r, a_i = pw_ref[0, 0, 0:1, :], pw_ref[0, 0, 1:2, :]
    first = row == 0
    er = er + jnp.where(first, a_r * c_r - a_i * c_i, 0.0)
    ei = ei + jnp.where(first, a_r * c_i + a_i * c_r, 0.0)
    for i in range(n_steps):
        d = 1 << i
        p_r, p_i = pw_ref[0, i, 0:1, :], pw_ref[0, i, 1:2, :]
        sr, si = _shift_rows(er, d, row), _shift_rows(ei, d, row)
        er, ei = er + p_r * sr - p_i * si, ei + p_r * si + p_i * sr
    prev_r = jnp.where(first, c_r, pltpu.roll(er, 1, 0))
    prev_i = jnp.where(first, c_i, pltpu.roll(ei, 1, 0))
    carry_ref[:, :SLAB_STATE] = er[nb - 1:nb, :]
    carry_ref[:, SLAB_STATE:] = ei[nb - 1:nb, :]
    prev = jnp.concatenate([prev_r, prev_i], axis=1).astype(BF16)
    y_ref[0] = y_intra + jnp.dot(prev, wout_ref[...], preferred_element_type=F32)


def _s5_post_kernel(x_ref, y_ref, g_ref, d_ref, wglu_ref, o_ref):
    nb = y_ref.shape[1]
    yj = jnp.stack([jnp.concatenate([y_ref[s, :, j * LANES:(j + 1) * LANES] for s in range(N_SLABS)], axis=1)
                    for j in range(S5_CHUNK)], axis=0)
    y = jnp.swapaxes(yj, 0, 1).reshape(S5_CHUNK * nb, D_MODEL)
    x = x_ref[...]
    z = jax.nn.gelu(y + d_ref[...] * _rms(x, g_ref[...]), approximate=True).astype(BF16)
    zz = jnp.dot(z, wglu_ref[...], preferred_element_type=F32)
    za, zb = zz[:, :D_MODEL], zz[:, D_MODEL:]
    o_ref[...] = x + za * (1.0 / (1.0 + jnp.exp(-zb)))


def _s5_layer(x, g_mix, cw1, cwout, pw, d_skip, w_glu):
    n_rows = x.shape[0] // S5_CHUNK
    nb = min(128, n_rows)
    nbp = min(64, n_rows)
    nbc = min(256, n_rows)
    n_steps = int(math.log2(nbc))
    assert n_rows % nb == 0 and n_rows % nbc == 0 and (1 << n_steps) == nbc and nbp % 8 == 0

    u = pl.pallas_call(
        _s5_pre_kernel,
        grid=(n_rows // nb,),
        in_specs=[pl.BlockSpec((S5_CHUNK * nb, D_MODEL), lambda i: (i, 0)),
                  pl.BlockSpec((1, D_MODEL), lambda i: (0, 0))],
        out_specs=pl.BlockSpec((N_SLABS, nb, SLAB_W), lambda i: (0, i, 0)),
        out_shape=jax.ShapeDtypeStruct((N_SLABS, n_rows, SLAB_W), BF16),
        compiler_params=_cparams(("parallel",)),
        name="s5_pre",
    )(x, g_mix)

    y = pl.pallas_call(
        functools.partial(_s5_core_kernel, n_steps=n_steps),
        grid=(N_SLABS, n_rows // nbc),
        in_specs=[pl.BlockSpec((1, nbc, SLAB_W), lambda s, r: (s, r, 0)),
                  pl.BlockSpec((1, SLAB_W, 2 * LANES), lambda s, r: (s, 0, 0)),
                  pl.BlockSpec((1, 2 * SLAB_STATE, LANES), lambda s, r: (s, 0, 0)),
                  pl.BlockSpec((2 * LANES, SLAB_W + 2 * SLAB_STATE), lambda s, r: (0, 0)),
                  pl.BlockSpec((1, n_steps, 2, SLAB_STATE), lambda s, r: (s, 0, 0, 0))],
        out_specs=pl.BlockSpec((1, nbc, SLAB_W), lambda s, r: (s, r, 0)),
        out_shape=jax.ShapeDtypeStruct((N_SLABS, n_rows, SLAB_W), F32),
        scratch_shapes=[pltpu.VMEM((1, 2 * SLAB_STATE), F32),
                        pltpu.VMEM((SLAB_W, SLAB_W + 2 * SLAB_STATE), BF16),
                        pltpu.VMEM((2 * SLAB_STATE, SLAB_W), BF16)],
        compiler_params=_cparams(("arbitrary", "arbitrary")),
        name="s5_core",
    )(u, cw1, cwout, _expansion_matrix(), pw)

    return pl.pallas_call(
        _s5_post_kernel,
        grid=(n_rows // nbp,),
        in_specs=[pl.BlockSpec((S5_CHUNK * nbp, D_MODEL), lambda i: (i, 0)),
                  pl.BlockSpec((N_SLABS, nbp, SLAB_W), lambda i: (0, i, 0)),
                  _const_spec((1, D_MODEL)),
                  _const_spec((1, D_MODEL)),
                  _const_spec((D_MODEL, 2 * D_MODEL))],
        out_specs=pl.BlockSpec((S5_CHUNK * nbp, D_MODEL), lambda i: (i, 0)),
        out_shape=jax.ShapeDtypeStruct(x.shape, F32),
        compiler_params=_cparams(("parallel",)),
        name="s5_post",
    )(x, y, g_mix, d_skip, w_glu)


def _ffn_body(h, g, w1_ref, w3_ref, w2_ref):
    hn = _rms(h, g).astype(BF16)
    acc = jnp.zeros(h.shape, F32)
    for f in range(FFN_HIDDEN // FFN_TILE):
        sl = slice(f * FFN_TILE, (f + 1) * FFN_TILE)
        a = jnp.dot(hn, w1_ref[:, sl], preferred_element_type=F32)
        b = jnp.dot(hn, w3_ref[:, sl], preferred_element_type=F32)
        t = (a * (1.0 / (1.0 + jnp.exp(-a))) * b).astype(BF16)
        acc = acc + jnp.dot(t, w2_ref[sl, :], preferred_element_type=F32)
    return h + acc


def _ffn_kernel(h_ref, g_ref, w1_ref, w3_ref, w2_ref, o_ref):
    o_ref[...] = _ffn_body(h_ref[...], g_ref[...], w1_ref, w3_ref, w2_ref)


def _attn_out_ffn_kernel(h_ref, a_ref, wo_ref, g_ref, w1_ref, w3_ref, w2_ref, gf_ref, o_ref):
    h = h_ref[...] + jnp.dot(a_ref[...], wo_ref[...], preferred_element_type=F32)
    o_ref[...] = _rms(_ffn_body(h, g_ref[...], w1_ref, w3_ref, w2_ref), gf_ref[...])


def _const_spec(shape):
    return pl.BlockSpec(shape, lambda i: (0,) * len(shape), pipeline_mode=pl.Buffered(1))


def _ffn_layer(h, g, w1, w3, w2):
    n, tm = h.shape[0], min(512, h.shape[0])
    return pl.pallas_call(
        _ffn_kernel,
        grid=(n // tm,),
        in_specs=[pl.BlockSpec((tm, D_MODEL), lambda i: (i, 0)),
                  _const_spec((1, D_MODEL)),
                  _const_spec((D_MODEL, FFN_HIDDEN)),
                  _const_spec((D_MODEL, FFN_HIDDEN)),
                  _const_spec((FFN_HIDDEN, D_MODEL))],
        out_specs=pl.BlockSpec((tm, D_MODEL), lambda i: (i, 0)),
        out_shape=jax.ShapeDtypeStruct(h.shape, F32),
        compiler_params=_cparams(("parallel",)),
        name="ffn",
    )(h, g, w1, w3, w2)


def _attn_out_ffn_layer(h, a, wo, g, w1, w3, w2, gf):
    n, tm = h.shape[0], min(512, h.shape[0])
    return pl.pallas_call(
        _attn_out_ffn_kernel,
        grid=(n // tm,),
        in_specs=[pl.BlockSpec((tm, D_MODEL), lambda i: (i, 0)),
                  pl.BlockSpec((tm, D_MODEL), lambda i: (i, 0)),
                  _const_spec((D_MODEL, D_MODEL)),
                  _const_spec((1, D_MODEL)),
                  _const_spec((D_MODEL, FFN_HIDDEN)),
                  _const_spec((D_MODEL, FFN_HIDDEN)),
                  _const_spec((FFN_HIDDEN, D_MODEL)),
                  _const_spec((1, D_MODEL))],
        out_specs=pl.BlockSpec((tm, D_MODEL), lambda i: (i, 0)),
        out_shape=jax.ShapeDtypeStruct(h.shape, F32),
        compiler_params=_cparams(("parallel",)),
        name="attn_out_ffn",
    )(h, a, wo, g, w1, w3, w2, gf)


def _rope(t, cos, sin_lo, sin_hi):
    return t * cos + pltpu.roll(t, LANES - ROT_DIM // 2, 1) * sin_lo + pltpu.roll(t, ROT_DIM // 2, 1) * sin_hi


def _pair_heads(x, hd, lane):
    hk = N_HEADS * HEAD_DIM
    c0 = (hd // 2) * LANES
    t1, t2 = x[:, c0:c0 + LANES], x[:, hk + c0:hk + c0 + LANES]
    if hd % 2 == 0:
        return jnp.where(lane < HEAD_DIM, t1, pltpu.roll(t2, HEAD_DIM, 1))
    return jnp.where(lane < HEAD_DIM, pltpu.roll(t1, HEAD_DIM, 1), t2)


def _proj_kernel(h_ref, gkv_ref, gq_ref, wkv_ref, wq_ref, cs_ref, sel_ref, one_ref, kk_ref, v_ref, qq_ref):
    h = h_ref[...]
    tab = jnp.dot(cs_ref[...], sel_ref[...], preferred_element_type=F32)
    cos, slo, shi = tab[:, :LANES] + one_ref[...], tab[:, LANES:2 * LANES], tab[:, 2 * LANES:]
    kv = jnp.dot(_rms(h, gkv_ref[...]).astype(BF16), wkv_ref[...], preferred_element_type=F32)
    q = jnp.dot(_rms(h, gq_ref[...]).astype(BF16), wq_ref[...], preferred_element_type=F32)
    lane = lax.broadcasted_iota(jnp.int32, (h.shape[0], LANES), 1)
    for hd in range(N_HEADS):
        kk_ref[hd] = _rope(_pair_heads(kv, hd, lane), cos, slo, shi).astype(BF16)
        qq_ref[hd] = (_rope(_pair_heads(q, hd, lane), cos, slo, shi) * Q_SCALE).astype(BF16)
        v_ref[hd] = kv[:, D_MODEL + hd * LANES:D_MODEL + (hd + 1) * LANES].astype(BF16)


def _proj_layer(h, gkv, gq, wkv, wq, cs, sel, one):
    n, tm = h.shape[0], min(512, h.shape[0])
    hshape = jax.ShapeDtypeStruct((N_HEADS, n, LANES), BF16)
    hspec = pl.BlockSpec((N_HEADS, tm, LANES), lambda i: (0, i, 0))
    return pl.pallas_call(
        _proj_kernel,
        grid=(n // tm,),
        in_specs=[pl.BlockSpec((tm, D_MODEL), lambda i: (i, 0)),
                  _const_spec((1, D_MODEL)), _const_spec((1, D_MODEL)),
                  _const_spec((D_MODEL, 2 * D_MODEL)), _const_spec((D_MODEL, D_MODEL)),
                  pl.BlockSpec((tm, cs.shape[1]), lambda i: (i, 0)),
                  _const_spec(sel.shape), _const_spec((1, LANES))],
        out_specs=[hspec, hspec, hspec],
        out_shape=[hshape, hshape, hshape],
        compiler_params=_cparams(("parallel",)),
        name="qkv_proj",
    )(h, gkv, gq, wkv, wq, cs, sel, one)


def _rope_tables(n):
    half = ROT_DIM // 2
    inv_freq = ROPE_THETA ** (-jnp.arange(half, dtype=F32) * 2.0 / ROT_DIM)
    ang = inv_freq[:, None] * jnp.arange(n, dtype=jnp.int32).astype(F32)[None, :]
    cs = jnp.concatenate([jnp.cos(ang), jnp.sin(ang)], axis=0).T
    hi = cs.astype(BF16)
    lo = (cs - hi.astype(F32)).astype(BF16)
    sel = np.zeros((ROT_DIM, 3 * LANES), np.float32)
    one = np.ones((1, LANES), np.float32)
    for l in range(LANES):
        d = l % HEAD_DIM
        if d < ROT_DIM:
            one[0, l] = 0.0
            sel[d % half, l] = 1.0
            if d < half:
                sel[half + d, LANES + l] = -1.0
            else:
                sel[half + d - half, 2 * LANES + l] = 1.0
    sel = np.concatenate([sel, sel], axis=0)
    return jnp.concatenate([hi, lo], axis=1), jnp.asarray(sel, BF16), jnp.asarray(one)


def _attn_kernel(lam_ref, qq_ref, kk_ref, v_ref, g_ref, o_ref, q2_ref, sa_ref, sb_ref, ma_ref, mb_ref,
                 ala_ref, alb_ref, mrun_ref, acco_ref, accl_ref, *, tq, tk, lam_init):
    qi = pl.program_id(1)
    q = qq_ref[0]
    lane = lax.broadcasted_iota(jnp.int32, q.shape, 1)
    zero = jnp.zeros_like(q)
    q2_ref[:tq, :] = jnp.where(lane < HEAD_DIM, q, zero)
    q2_ref[tq:, :] = jnp.where(lane >= HEAD_DIM, q, zero)
    mrun_ref[...] = jnp.full(mrun_ref.shape, -jnp.inf, F32)
    acco_ref[...] = jnp.zeros(acco_ref.shape, F32)
    accl_ref[...] = jnp.zeros(accl_ref.shape, F32)
    buf_a = (sa_ref, ma_ref, ala_ref)
    buf_b = (sb_ref, mb_ref, alb_ref)

    def scores(blk, buf, mask_off):
        s_ref, m_ref, al_ref = buf
        k = kk_ref[0, pl.ds(pl.multiple_of(blk * tk, tk), tk), :]
        s = lax.dot_general(q2_ref[...], k, (((1,), (1,)), ((), ())), preferred_element_type=F32)
        if mask_off is not None:
            r = lax.broadcasted_iota(jnp.int32, s.shape, 0)
            c = lax.broadcasted_iota(jnp.int32, s.shape, 1)
            r = jnp.where(r >= tq, r - tq, r)
            s = jnp.where(((c + mask_off) // CHUNK) <= (r // CHUNK), s, -jnp.inf)
        s_ref[...] = s
        m_prev = mrun_ref[...]
        m_new = jnp.maximum(m_prev, jnp.max(s, axis=1, keepdims=True))
        al_ref[...] = jnp.exp2(m_prev - m_new)
        m_ref[...] = m_new
        mrun_ref[...] = m_new

    def accumulate(blk, buf):
        s_ref, m_ref, al_ref = buf
        v = v_ref[0, pl.ds(pl.multiple_of(blk * tk, tk), tk), :]
        v1 = jnp.concatenate([v, jnp.ones_like(v)], axis=1)
        m = m_ref[...]
        p = jnp.concatenate([jnp.exp2(s_ref[:, j * LANES:(j + 1) * LANES] - m) for j in range(tk // LANES)],
                            axis=1).astype(BF16)
        pv = jnp.dot(p, v1, preferred_element_type=F32)
        al = al_ref[...]
        acco_ref[...] = al * acco_ref[...] + pv[:, :LANES]
        accl_ref[...] = al * accl_ref[...] + pv[:, LANES:]

    @pl.when(qi == 0)
    def _():
        scores(0, buf_a, 0)

    @pl.when(qi > 0)
    def _():
        scores(0, buf_a, None)

    def pair(t, next_mask):
        scores(2 * t + 1, buf_b, None)
        accumulate(2 * t, buf_a)
        scores(2 * t + 2, buf_a, next_mask)
        accumulate(2 * t + 1, buf_b)

    def body(t, c):
        pair(t, None)
        return c

    lax.fori_loop(0, qi - 1, body, 0)

    @pl.when(qi > 0)
    def _():
        pair(qi - 1, 0)

    scores(2 * qi + 1, buf_b, tk)
    accumulate(2 * qi, buf_a)
    accumulate(2 * qi + 1, buf_b)

    o = acco_ref[...] / accl_ref[...]
    o = o[:tq] - lam_ref[0] * o[tq:]
    o_ref[...] = (_rms(o, g_ref[...]) * (1.0 - lam_init)).astype(o_ref.dtype)


def _attn_layer(lam, qq, kk, v, g, lam_init):
    n = qq.shape[1]
    tq = min(1024, n)
    tk = tq // 2
    assert n % tq == 0 and tk % CHUNK == 0 and tk % LANES == 0
    stat = pltpu.VMEM((2 * tq, LANES), F32)
    return pl.pallas_call(
        functools.partial(_attn_kernel, tq=tq, tk=tk, lam_init=lam_init),
        grid=(N_HEADS, n // tq),
        in_specs=[pl.BlockSpec(memory_space=pltpu.SMEM),
                  pl.BlockSpec((1, tq, LANES), lambda h, i: (h, i, 0)),
                  pl.BlockSpec((1, n, LANES), lambda h, i: (h, 0, 0)),
                  pl.BlockSpec((1, n, LANES), lambda h, i: (h, 0, 0)),
                  pl.BlockSpec((1, LANES), lambda h, i: (0, 0))],
        out_specs=pl.BlockSpec((tq, LANES), lambda h, i: (i, h)),
        out_shape=jax.ShapeDtypeStruct((n, N_HEADS * V_DIM), BF16),
        scratch_shapes=[pltpu.VMEM((2 * tq, LANES), BF16),
                        pltpu.VMEM((2 * tq, tk), F32), pltpu.VMEM((2 * tq, tk), F32),
                        stat, stat, stat, stat, stat, stat, stat],
        compiler_params=_cparams(("parallel", "arbitrary")),
        name="diff_attn",
    )(lam, qq, kk, v, g)


def kernel(x, norm_mix_g, norm_ffn_g, ffn_w1, ffn_w3, ffn_w2, ssm_lam_re, ssm_lam_im, ssm_log_dt, ssm_b_re, ssm_b_im, ssm_c_re, ssm_c_im, ssm_d, ssm_w_glu, kv_norm_g, w_kv, attn_w_q, attn_lq1, attn_lk1, attn_lq2, attn_lk2, attn_subln_g, attn_w_o, final_norm_g):
    bsz, L, _ = x.shape
    assert bsz == 1 and L % (S5_CHUNK * 8) == 0
    row = lambda t: t.reshape(1, -1).astype(F32)

    cw1, cwout, a_chunk = _s5_weights(ssm_lam_re[0], ssm_lam_im[0], ssm_log_dt[0], ssm_b_re[0], ssm_b_im[0],
                                      ssm_c_re[0], ssm_c_im[0])
    n_rows = L // S5_CHUNK
    pw = _scan_powers(a_chunk, int(math.log2(min(256, n_rows))))
    h = _s5_layer(x.reshape(L, D_MODEL), row(norm_mix_g[0]), cw1, cwout, pw, row(ssm_d[0]),
                  ssm_w_glu[0].astype(BF16))
    h = _ffn_layer(h, row(norm_ffn_g[0]), ffn_w1[0].astype(BF16), ffn_w3[0].astype(BF16), ffn_w2[0].astype(BF16))

    hk = N_HEADS * HEAD_DIM
    cs, sel, one = _rope_tables(L)
    kk, v, qq = _proj_layer(h, row(kv_norm_g), row(norm_mix_g[1]), w_kv.astype(BF16), attn_w_q[0].astype(BF16),
                            cs, sel, one)
    lam_init = 0.8 - 0.6 * math.exp(-0.3 * N_A_LAYERS)
    lam = (jnp.exp(jnp.sum(attn_lq1[0].astype(F32) * attn_lk1[0].astype(F32)))
           - jnp.exp(jnp.sum(attn_lq2[0].astype(F32) * attn_lk2[0].astype(F32))) + lam_init).reshape(1)
    a = _attn_layer(lam, qq, kk, v, row(attn_subln_g[0]), lam_init)
    out = _attn_out_ffn_layer(h, a, attn_w_o[0].astype(BF16), row(norm_ffn_g[1]), ffn_w1[1].astype(BF16),
                              ffn_w3[1].astype(BF16), ffn_w2[1].astype(BF16), row(final_norm_g))
    return out.reshape(1, L, D_MODEL)
```

```python
import functools
import math

import jax
import jax.numpy as jnp
import numpy as np
from jax import lax
from jax.experimental import pallas as pl
from jax.experimental.pallas import tpu as pltpu

F32 = jnp.float32
BF16 = jnp.bfloat16

D_MODEL = 1024
CHUNK = 64
SSM_GROUP = 16
N_GROUPS = D_MODEL // SSM_GROUP
SSM_STATE = 64
N_HEADS = 8
HEAD_DIM = 64
V_DIM = 2 * HEAD_DIM
ROT_DIM = HEAD_DIM // 4
ROPE_THETA = 500000.0
FFN_HIDDEN = 2816
EPS = 1e-6
N_A_LAYERS = 1

LANES = 128
S5_CHUNK = 8
N_SLABS = D_MODEL // LANES
GROUPS_PER_SLAB = LANES // SSM_GROUP
SLAB_STATE = GROUPS_PER_SLAB * SSM_STATE
SLAB_W = S5_CHUNK * LANES
FFN_TILE = 256
Q_SCALE = HEAD_DIM ** -0.5 * math.log2(math.e)
VMEM_LIMIT = 56 * 1024 * 1024


def _cparams(sem):
    return pltpu.CompilerParams(dimension_semantics=sem, vmem_limit_bytes=VMEM_LIMIT)


def _rms(x, g):
    return x * lax.rsqrt(jnp.mean(x * x, axis=-1, keepdims=True) + EPS) * g


def _s5_weights(lam_re, lam_im, log_dt, b_re, b_im, c_re, c_im):
    hi = lax.Precision.HIGHEST
    lr, li = lam_re.astype(F32), lam_im.astype(F32)
    dt = jnp.exp(log_dt.astype(F32))[:, None]
    mag = jnp.exp(lr * dt)
    ar, ai = mag * jnp.cos(li * dt), mag * jnp.sin(li * dt)
    nr, ni = ar - 1.0, ai
    den = lr * lr + li * li
    fr = (nr * lr + ni * li) / den
    fi = (ni * lr - nr * li) / den
    br, bi = b_re.astype(F32), b_im.astype(F32)
    bbr = fr[..., None] * br - fi[..., None] * bi
    bbi = fr[..., None] * bi + fi[..., None] * br
    cr, ci = c_re.astype(F32), c_im.astype(F32)

    prs, pis = [jnp.ones_like(ar)], [jnp.zeros_like(ar)]
    for _ in range(S5_CHUNK):
        prs.append(prs[-1] * ar - pis[-1] * ai)
        pis.append(prs[-2] * ai + pis[-1] * ar)
    pr = jnp.stack(prs)
    pi = jnp.stack(pis)

    T = S5_CHUNK
    kr = pr[:T, :, :, None] * bbr[None] - pi[:T, :, :, None] * bbi[None]
    ki = pr[:T, :, :, None] * bbi[None] + pi[:T, :, :, None] * bbr[None]
    klag = (jnp.einsum('gdp,kgpc->kgcd', cr, kr, precision=hi)
            - jnp.einsum('gdp,kgpc->kgcd', ci, ki, precision=hi))
    tt = jnp.arange(T)
    lag = tt[None, :] - tt[:, None]
    toe = jnp.where((lag >= 0)[:, :, None, None, None],
                    klag[jnp.clip(lag, 0, T - 1)], 0.0)
    win_r = kr[::-1]
    win_i = ki[::-1]
    p1r, p1i = pr[1:], pi[1:]
    out_r = cr[None] * p1r[:, :, None, :] - ci[None] * p1i[:, :, None, :]
    out_i = -cr[None] * p1i[:, :, None, :] - ci[None] * p1r[:, :, None, :]

    S, Gs = N_SLABS, GROUPS_PER_SLAB
    c_toe = toe.reshape(T, T, S, Gs, SSM_GROUP, SSM_GROUP).transpose(2, 0, 3, 4, 1, 5).reshape(S, SLAB_W, LANES)
    to_rows = lambda w: w.reshape(T, S, Gs, SSM_STATE, SSM_GROUP).transpose(1, 0, 2, 4, 3).reshape(S, SLAB_W, SSM_STATE)
    cw1 = jnp.concatenate([c_toe, to_rows(win_r), to_rows(win_i)], axis=-1).astype(BF16)
    to_out = lambda w: w.reshape(T, S, Gs, SSM_GROUP, SSM_STATE).transpose(1, 2, 4, 0, 3).reshape(S, SLAB_STATE, LANES)
    cwout = jnp.concatenate([to_out(out_r), to_out(out_i)], axis=1).astype(BF16)
    a_chunk = (pr[T].reshape(S, SLAB_STATE), pi[T].reshape(S, SLAB_STATE))
    return cw1, cwout, a_chunk


def _expansion_matrix():
    e = np.zeros((2 * LANES, SLAB_W + 2 * SLAB_STATE), np.float32)
    for h in range(GROUPS_PER_SLAB):
        for t in range(S5_CHUNK):
            for d in range(SSM_GROUP):
                e[t * SSM_GROUP + d, t * LANES + h * SSM_GROUP + d] = 1.0
        for p in range(SSM_STATE):
            e[LANES + p, SLAB_W + h * SSM_STATE + p] = 1.0
            e[LANES + SSM_STATE + p, SLAB_W + SLAB_STATE + h * SSM_STATE + p] = 1.0
    return jnp.asarray(e, BF16)


def _scan_powers(a_chunk, n_steps):
    r, i = a_chunk
    rows = []
    for _ in range(n_steps):
        rows.append(jnp.stack([r, i], axis=1))
        r, i = r * r - i * i, 2.0 * r * i
    return jnp.stack(rows, axis=1)


def _s5_pre_kernel(x_ref, g_ref, u_ref):
    nb = u_ref.shape[1]
    u = _rms(x_ref[...], g_ref[...])
    uj = jnp.swapaxes(u.reshape(nb, S5_CHUNK, D_MODEL), 0, 1)
    for j in range(S5_CHUNK):
        for s in range(N_SLABS):
            u_ref[s, :, j * LANES:(j + 1) * LANES] = uj[j, :, s * LANES:(s + 1) * LANES].astype(BF16)


def _shift_rows(x, d, row):
    n = x.shape[0]
    if d % 8 == 0:
        return jnp.concatenate([jnp.zeros((d, x.shape[1]), x.dtype), x[:n - d]], axis=0)
    return jnp.where(row >= d, pltpu.roll(x, d, 0), 0.0)


def _expand_block_diag(c, e_ref, w_ref, row_period, row_group):
    n_rows = c.shape[0]
    tile = SLAB_STATE
    for ct in range(w_ref.shape[1] // tile):
        x = jnp.dot(c, e_ref[:c.shape[1], ct * tile:(ct + 1) * tile], preferred_element_type=F32)
        row = lax.broadcasted_iota(jnp.int32, (n_rows, tile), 0)
        col = lax.broadcasted_iota(jnp.int32, (n_rows, tile), 1)
        if ct * tile < SLAB_W:
            col_g = (col % LANES) // SSM_GROUP
        else:
            col_g = col // SSM_STATE
        keep = (row % row_period) // row_group == col_g
        w_ref[:, ct * tile:(ct + 1) * tile] = jnp.where(keep, x, 0.0).astype(w_ref.dtype)


def _s5_core_kernel(u_ref, cw1_ref, cwout_ref, e_ref, pw_ref, y_ref, carry_ref, w1_ref, wout_ref, *, n_steps):
    @pl.when(pl.program_id(1) == 0)
    def _():
        carry_ref[...] = jnp.zeros_like(carry_ref)
        _expand_block_diag(cw1_ref[0], e_ref, w1_ref, LANES, SSM_GROUP)
        _expand_block_diag(cwout_ref[0], e_ref, wout_ref, SLAB_STATE, SSM_STATE)

    res = jnp.dot(u_ref[0], w1_ref[...], preferred_element_type=F32)
    nb = res.shape[0]
    y_intra = res[:, :SLAB_W]
    er = res[:, SLAB_W:SLAB_W + SLAB_STATE]
    ei = res[:, SLAB_W + SLAB_STATE:]
    row = lax.broadcasted_iota(jnp.int32, (nb, SLAB_STATE), 0)
    c_r = carry_ref[:, :SLAB_STATE]
    c_i = carry_ref[:, SLAB_STATE:]
    a_r, a_i = pw_ref[0, 0, 0:1, :], pw_ref[0, 0, 1:2, :]
    first = row == 0
    er = er + jnp.where(first, a_r * c_r - a_i * c_i, 0.0)
    ei = ei + jnp.where(first, a_r * c_i + a_i * c_r, 0.0)
    for i in range(n_steps):
        d = 1 << i
        p_r, p_i = pw_ref[0, i, 0:1, :], pw_ref[0, i, 1:2, :]
        sr, si = _shift_rows(er, d, row), _shift_rows(ei, d, row)
        er, ei = er + p_r * sr - p_i * si, ei + p_r * si + p_i * sr
    prev_r = jnp.where(first, c_r, pltpu.roll(er, 1, 0))
    prev_i = jnp.where(first, c_i, pltpu.roll(ei, 1, 0))
    carry_ref[:, :SLAB_STATE] = er[nb - 1:nb, :]
    carry_ref[:, SLAB_STATE:] = ei[nb - 1:nb, :]
    prev = jnp.concatenate([prev_r, prev_i], axis=1).astype(BF16)
    y_ref[0] = y_intra + jnp.dot(prev, wout_ref[...], preferred_element_type=F32)


def _s5_post_kernel(x_ref, y_ref, g_ref, d_ref, wglu_ref, o_ref):
    nb = y_ref.shape[1]
    yj = jnp.stack([jnp.concatenate([y_ref[s, :, j * LANES:(j + 1) * LANES] for s in range(N_SLABS)], axis=1)
                    for j in range(S5_CHUNK)], axis=0)
    y = jnp.swapaxes(yj, 0, 1).reshape(S5_CHUNK * nb, D_MODEL)
    x = x_ref[...]
    z = jax.nn.gelu(y + d_ref[...] * _rms(x, g_ref[...]), approximate=True).astype(BF16)
    zz = jnp.dot(z, wglu_ref[...], preferred_element_type=F32)
    za, zb = zz[:, :D_MODEL], zz[:, D_MODEL:]
    o_ref[...] = x + za * (1.0 / (1.0 + jnp.exp(-zb)))


def _s5_layer(x, g_mix, cw1, cwout, pw, d_skip, w_glu):
    n_rows = x.shape[0] // S5_CHUNK
    nb = min(128, n_rows)
    nbp = min(64, n_rows)
    nbc = min(256, n_rows)
    n_steps = int(math.log2(nbc))
    assert n_rows % nb == 0 and n_rows % nbc == 0 and (1 << n_steps) == nbc and nbp % 8 == 0

    u = pl.pallas_call(
        _s5_pre_kernel,
        grid=(n_rows // nb,),
        in_specs=[pl.BlockSpec((S5_CHUNK * nb, D_MODEL), lambda i: (i, 0)),
                  pl.BlockSpec((1, D_MODEL), lambda i: (0, 0))],
        out_specs=pl.BlockSpec((N_SLABS, nb, SLAB_W), lambda i: (0, i, 0)),
        out_shape=jax.ShapeDtypeStruct((N_SLABS, n_rows, SLAB_W), BF16),
        compiler_params=_cparams(("parallel",)),
        name="s5_pre",
    )(x, g_mix)

    y = pl.pallas_call(
        functools.partial(_s5_core_kernel, n_steps=n_steps),
        grid=(N_SLABS, n_rows // nbc),
        in_specs=[pl.BlockSpec((1, nbc, SLAB_W), lambda s, r: (s, r, 0)),
                  pl.BlockSpec((1, SLAB_W, 2 * LANES), lambda s, r: (s, 0, 0)),
                  pl.BlockSpec((1, 2 * SLAB_STATE, LANES), lambda s, r: (s, 0, 0)),
                  pl.BlockSpec((2 * LANES, SLAB_W + 2 * SLAB_STATE), lambda s, r: (0, 0)),
                  pl.BlockSpec((1, n_steps, 2, SLAB_STATE), lambda s, r: (s, 0, 0, 0))],
        out_specs=pl.BlockSpec((1, nbc, SLAB_W), lambda s, r: (s, r, 0)),
        out_shape=jax.ShapeDtypeStruct((N_SLABS, n_rows, SLAB_W), F32),
        scratch_shapes=[pltpu.VMEM((1, 2 * SLAB_STATE), F32),
                        pltpu.VMEM((SLAB_W, SLAB_W + 2 * SLAB_STATE), BF16),
                        pltpu.VMEM((2 * SLAB_STATE, SLAB_W), BF16)],
        compiler_params=_cparams(("arbitrary", "arbitrary")),
        name="s5_core",
    )(u, cw1, cwout, _expansion_matrix(), pw)

    return pl.pallas_call(
        _s5_post_kernel,
        grid=(n_rows // nbp,),
        in_specs=[pl.BlockSpec((S5_CHUNK * nbp, D_MODEL), lambda i: (i, 0)),
                  pl.BlockSpec((N_SLABS, nbp, SLAB_W), lambda i: (0, i, 0)),
                  _const_spec((1, D_MODEL)),
                  _const_spec((1, D_MODEL)),
                  _const_spec((D_MODEL, 2 * D_MODEL))],
        out_specs=pl.BlockSpec((S5_CHUNK * nbp, D_MODEL), lambda i: (i, 0)),
        out_shape=jax.ShapeDtypeStruct(x.shape, F32),
        compiler_params=_cparams(("parallel",)),
        name="s5_post",
    )(x, y, g_mix, d_skip, w_glu)


def _ffn_body(h, g, w1_ref, w3_ref, w2_ref):
    hn = _rms(h, g).astype(BF16)
    acc = jnp.zeros(h.shape, F32)
    for f in range(FFN_HIDDEN // FFN_TILE):
        sl = slice(f * FFN_TILE, (f + 1) * FFN_TILE)
        a = jnp.dot(hn, w1_ref[:, sl], preferred_element_type=F32)
        b = jnp.dot(hn, w3_ref[:, sl], preferred_element_type=F32)
        t = (a * (1.0 / (1.0 + jnp.exp(-a))) * b).astype(BF16)
        acc = acc + jnp.dot(t, w2_ref[sl, :], preferred_element_type=F32)
    return h + acc


def _ffn_kernel(h_ref, g_ref, w1_ref, w3_ref, w2_ref, o_ref):
    o_ref[...] = _ffn_body(h_ref[...], g_ref[...], w1_ref, w3_ref, w2_ref)


def _attn_out_ffn_kernel(h_ref, a_ref, wo_ref, g_ref, w1_ref, w3_ref, w2_ref, gf_ref, o_ref):
    h = h_ref[...] + jnp.dot(a_ref[...], wo_ref[...], preferred_element_type=F32)
    o_ref[...] = _rms(_ffn_body(h, g_ref[...], w1_ref, w3_ref, w2_ref), gf_ref[...])


def _const_spec(shape):
    return pl.BlockSpec(shape, lambda i: (0,) * len(shape), pipeline_mode=pl.Buffered(1))


def _ffn_layer(h, g, w1, w3, w2):
    n, tm = h.shape[0], min(512, h.shape[0])
    return pl.pallas_call(
        _ffn_kernel,
        grid=(n // tm,),
        in_specs=[pl.BlockSpec((tm, D_MODEL), lambda i: (i, 0)),
                  _const_spec((1, D_MODEL)),
                  _const_spec((D_MODEL, FFN_HIDDEN)),
                  _const_spec((D_MODEL, FFN_HIDDEN)),
                  _const_spec((FFN_HIDDEN, D_MODEL))],
        out_specs=pl.BlockSpec((tm, D_MODEL), lambda i: (i, 0)),
        out_shape=jax.ShapeDtypeStruct(h.shape, F32),
        compiler_params=_cparams(("parallel",)),
        name="ffn",
    )(h, g, w1, w3, w2)


def _attn_out_ffn_layer(h, a, wo, g, w1, w3, w2, gf):
    n, tm = h.shape[0], min(512, h.shape[0])
    return pl.pallas_call(
        _attn_out_ffn_kernel,
        grid=(n // tm,),
        in_specs=[pl.BlockSpec((tm, D_MODEL), lambda i: (i, 0)),
                  pl.BlockSpec((tm, D_MODEL), lambda i: (i, 0)),
                  _const_spec((D_MODEL, D_MODEL)),
                  _const_spec((1, D_MODEL)),
                  _const_spec((D_MODEL, FFN_HIDDEN)),
                  _const_spec((D_MODEL, FFN_HIDDEN)),
                  _const_spec((FFN_HIDDEN, D_MODEL)),
                  _const_spec((1, D_MODEL))],
        out_specs=pl.BlockSpec((tm, D_MODEL), lambda i: (i, 0)),
        out_shape=jax.ShapeDtypeStruct(h.shape, F32),
        compiler_params=_cparams(("parallel",)),
        name="attn_out_ffn",
    )(h, a, wo, g, w1, w3, w2, gf)


def _rope(t, cos, sin_lo, sin_hi):
    return t * cos + pltpu.roll(t, LANES - ROT_DIM // 2, 1) * sin_lo + pltpu.roll(t, ROT_DIM // 2, 1) * sin_hi


def _pair_heads(x, hd, lane):
    hk = N_HEADS * HEAD_DIM
    c0 = (hd // 2) * LANES
    t1, t2 = x[:, c0:c0 + LANES], x[:, hk + c0:hk + c0 + LANES]
    if hd % 2 == 0:
        return jnp.where(lane < HEAD_DIM, t1, pltpu.roll(t2, HEAD_DIM, 1))
    return jnp.where(lane < HEAD_DIM, pltpu.roll(t1, HEAD_DIM, 1), t2)


def _proj_kernel(h_ref, gkv_ref, gq_ref, wkv_ref, wq_ref, cs_ref, sel_ref, one_ref, kk_ref, v_ref, qq_ref):
    h = h_ref[...]
    tab = jnp.dot(cs_ref[...], sel_ref[...], preferred_element_type=F32)
    cos, slo, shi = tab[:, :LANES] + one_ref[...], tab[:, LANES:2 * LANES], tab[:, 2 * LANES:]
    kv = jnp.dot(_rms(h, gkv_ref[...]).astype(BF16), wkv_ref[...], preferred_element_type=F32)
    q = jnp.dot(_rms(h, gq_ref[...]).astype(BF16), wq_ref[...], preferred_element_type=F32)
    lane = lax.broadcasted_iota(jnp.int32, (h.shape[0], LANES), 1)
    for hd in range(N_HEADS):
        kk_ref[hd] = _rope(_pair_heads(kv, hd, lane), cos, slo, shi).astype(BF16)
        qq_ref[hd] = (_rope(_pair_heads(q, hd, lane), cos, slo, shi) * Q_SCALE).astype(BF16)
        v_ref[hd] = kv[:, D_MODEL + hd * LANES:D_MODEL + (hd + 1) * LANES].astype(BF16)


def _proj_layer(h, gkv, gq, wkv, wq, cs, sel, one):
    n, tm = h.shape[0], min(512, h.shape[0])
    hshape = jax.ShapeDtypeStruct((N_HEADS, n, LANES), BF16)
    hspec = pl.BlockSpec((N_HEADS, tm, LANES), lambda i: (0, i, 0))
    return pl.pallas_call(
        _proj_kernel,
        grid=(n // tm,),
        in_specs=[pl.BlockSpec((tm, D_MODEL), lambda i: (i, 0)),
                  _const_spec((1, D_MODEL)), _const_spec((1, D_MODEL)),
                  _const_spec((D_MODEL, 2 * D_MODEL)), _const_spec((D_MODEL, D_MODEL)),
                  pl.BlockSpec((tm, cs.shape[1]), lambda i: (i, 0)),
                  _const_spec(sel.shape), _const_spec((1, LANES))],
        out_specs=[hspec, hspec, hspec],
        out_shape=[hshape, hshape, hshape],
        compiler_params=_cparams(("parallel",)),
        name="qkv_proj",
    )(h, gkv, gq, wkv, wq, cs, sel, one)


def _rope_tables(n):
    half = ROT_DIM // 2
    inv_freq = ROPE_THETA ** (-jnp.arange(half, dtype=F32) * 2.0 / ROT_DIM)
    ang = inv_freq[:, None] * jnp.arange(n, dtype=jnp.int32).astype(F32)[None, :]
    cs = jnp.concatenate([jnp.cos(ang), jnp.sin(ang)], axis=0).T
    hi = cs.astype(BF16)
    lo = (cs - hi.astype(F32)).astype(BF16)
    sel = np.zeros((ROT_DIM, 3 * LANES), np.float32)
    one = np.ones((1, LANES), np.float32)
    for l in range(LANES):
        d = l % HEAD_DIM
        if d < ROT_DIM:
            one[0, l] = 0.0
            sel[d % half, l] = 1.0
            if d < half:
                sel[half + d, LANES + l] = -1.0
            else:
                sel[half + d - half, 2 * LANES + l] = 1.0
    sel = np.concatenate([sel, sel], axis=0)
    return jnp.concatenate([hi, lo], axis=1), jnp.asarray(sel, BF16), jnp.asarray(one)


def _attn_kernel(lam_ref, qq_ref, kk_ref, v_ref, g_ref, o_ref, q2_ref, sa_ref, sb_ref, ma_ref, mb_ref,
                 ala_ref, alb_ref, mrun_ref, acco_ref, accl_ref, *, tq, tk, lam_init):
    qi = pl.program_id(1)
    lane = lax.broadcasted_iota(jnp.int32, (tk, LANES), 1)
    zero = jnp.zeros((tk, LANES), BF16)
    for half in range(2):
        q = qq_ref[0, half * tk:(half + 1) * tk, :]
        q2_ref[(2 * half) * tk:(2 * half + 1) * tk, :] = jnp.where(lane < HEAD_DIM, q, zero)
        q2_ref[(2 * half + 1) * tk:(2 * half + 2) * tk, :] = jnp.where(lane >= HEAD_DIM, q, zero)
    mrun_ref[...] = jnp.full(mrun_ref.shape, -jnp.inf, F32)
    acco_ref[...] = jnp.zeros(acco_ref.shape, F32)
    accl_ref[...] = jnp.zeros(accl_ref.shape, F32)
    buf_a = (sa_ref, ma_ref, ala_ref)
    buf_b = (sb_ref, mb_ref, alb_ref)
    late = slice(2 * tk, 4 * tk)

    def causal(s):
        r = lax.broadcasted_iota(jnp.int32, s.shape, 0)
        c = lax.broadcasted_iota(jnp.int32, s.shape, 1)
        return jnp.where((c // CHUNK) <= ((r % tk) // CHUNK), s, -jnp.inf)

    def stats(s, rows, buf):
        s_ref, m_ref, al_ref = buf
        s_ref[rows, :] = s
        m_prev = mrun_ref[rows, :]
        m_new = jnp.maximum(m_prev, jnp.max(s, axis=1, keepdims=True))
        al_ref[rows, :] = jnp.exp2(m_prev - m_new)
        m_ref[rows, :] = m_new
        mrun_ref[rows, :] = m_new

    def scores(blk, buf, kind):
        k = kk_ref[0, pl.ds(pl.multiple_of(blk * tk, tk), tk), :]
        dims = (((1,), (1,)), ((), ()))
        if kind == "last":
            stats(causal(lax.dot_general(q2_ref[late, :], k, dims, preferred_element_type=F32)), late, buf)
            return
        s = lax.dot_general(q2_ref[...], k, dims, preferred_element_type=F32)
        if kind == "diag":
            stats(causal(s[:2 * tk]), slice(0, 2 * tk), buf)
            stats(s[2 * tk:], late, buf)
        else:
            stats(s, slice(None), buf)

    def accumulate(blk, buf, rows=slice(None)):
        s_ref, m_ref, al_ref = buf
        v = v_ref[0, pl.ds(pl.multiple_of(blk * tk, tk), tk), :]
        v1 = jnp.concatenate([v, jnp.ones_like(v)], axis=1)
        m = m_ref[rows, :]
        p = jnp.concatenate([jnp.exp2(s_ref[rows, j * LANES:(j + 1) * LANES] - m) for j in range(tk // LANES)],
                            axis=1).astype(BF16)
        pv = jnp.dot(p, v1, preferred_element_type=F32)
        al = al_ref[rows, :]
        acco_ref[rows, :] = al * acco_ref[rows, :] + pv[:, :LANES]
        accl_ref[rows, :] = al * accl_ref[rows, :] + pv[:, LANES:]

    @pl.when(qi == 0)
    def _():
        scores(0, buf_a, "diag")

    @pl.when(qi > 0)
    def _():
        scores(0, buf_a, "full")

    def pair(t, next_kind):
        scores(2 * t + 1, buf_b, "full")
        accumulate(2 * t, buf_a)
        scores(2 * t + 2, buf_a, next_kind)
        accumulate(2 * t + 1, buf_b)

    n_loop = jnp.maximum(qi - 1, 0)
    odd = n_loop % 2

    @pl.when(odd == 1)
    def _():
        pair(0, "full")

    def body(u, c):
        t = odd + 2 * u
        pair(t, "full")
        pair(t + 1, "full")
        return c

    lax.fori_loop(0, n_loop // 2, body, 0)

    @pl.when(qi > 0)
    def _():
        pair(qi - 1, "diag")

    scores(2 * qi + 1, buf_b, "last")
    accumulate(2 * qi, buf_a)
    accumulate(2 * qi + 1, buf_b, late)

    o = acco_ref[...] / accl_ref[...]
    lam = lam_ref[0]
    o = jnp.concatenate([o[:tk] - lam * o[tk:2 * tk], o[2 * tk:3 * tk] - lam * o[3 * tk:]], axis=0)
    o_ref[...] = (_rms(o, g_ref[...]) * (1.0 - lam_init)).astype(o_ref.dtype)


def _attn_layer(lam, qq, kk, v, g, lam_init):
    n = qq.shape[1]
    tq = min(1024, n)
    tk = tq // 2
    assert n % tq == 0 and tk % CHUNK == 0 and tk % LANES == 0
    stat = pltpu.VMEM((2 * tq, LANES), F32)
    return pl.pallas_call(
        functools.partial(_attn_kernel, tq=tq, tk=tk, lam_init=lam_init),
        grid=(N_HEADS, n // tq),
        in_specs=[pl.BlockSpec(memory_space=pltpu.SMEM),
                  pl.BlockSpec((1, tq, LANES), lambda h, i: (h, i, 0)),
                  pl.BlockSpec((1, n, LANES), lambda h, i: (h, 0, 0)),
                  pl.BlockSpec((1, n, LANES), lambda h, i: (h, 0, 0)),
                  pl.BlockSpec((1, LANES), lambda h, i: (0, 0))],
        out_specs=pl.BlockSpec((tq, LANES), lambda h, i: (i, h)),
        out_shape=jax.ShapeDtypeStruct((n, N_HEADS * V_DIM), BF16),
        scratch_shapes=[pltpu.VMEM((2 * tq, LANES), BF16),
                        pltpu.VMEM((2 * tq, tk), F32), pltpu.VMEM((2 * tq, tk), F32),
                        stat, stat, stat, stat, stat, stat, stat],
        compiler_params=_cparams(("parallel", "arbitrary")),
        name="diff_attn",
    )(lam, qq, kk, v, g)


def kernel(x, norm_mix_g, norm_ffn_g, ffn_w1, ffn_w3, ffn_w2, ssm_lam_re, ssm_lam_im, ssm_log_dt, ssm_b_re, ssm_b_im, ssm_c_re, ssm_c_im, ssm_d, ssm_w_glu, kv_norm_g, w_kv, attn_w_q, attn_lq1, attn_lk1, attn_lq2, attn_lk2, attn_subln_g, attn_w_o, final_norm_g):
    bsz, L, _ = x.shape
    assert bsz == 1 and L % (S5_CHUNK * 8) == 0
    row = lambda t: t.reshape(1, -1).astype(F32)

    cw1, cwout, a_chunk = _s5_weights(ssm_lam_re[0], ssm_lam_im[0], ssm_log_dt[0], ssm_b_re[0], ssm_b_im[0],
                                      ssm_c_re[0], ssm_c_im[0])
    n_rows = L // S5_CHUNK
    pw = _scan_powers(a_chunk, int(math.log2(min(256, n_rows))))
    h = _s5_layer(x.reshape(L, D_MODEL), row(norm_mix_g[0]), cw1, cwout, pw, row(ssm_d[0]),
                  ssm_w_glu[0].astype(BF16))
    h = _ffn_layer(h, row(norm_ffn_g[0]), ffn_w1[0].astype(BF16), ffn_w3[0].astype(BF16), ffn_w2[0].astype(BF16))

    hk = N_HEADS * HEAD_DIM
    cs, sel, one = _rope_tables(L)
    kk, v, qq = _proj_layer(h, row(kv_norm_g), row(norm_mix_g[1]), w_kv.astype(BF16), attn_w_q[0].astype(BF16),
                            cs, sel, one)
    lam_init = 0.8 - 0.6 * math.exp(-0.3 * N_A_LAYERS)
    lam = (jnp.exp(jnp.sum(attn_lq1[0].astype(F32) * attn_lk1[0].astype(F32)))
           - jnp.exp(jnp.sum(attn_lq2[0].astype(F32) * attn_lk2[0].astype(F32))) + lam_init).reshape(1)
    a = _attn_layer(lam, qq, kk, v, row(attn_subln_g[0]), lam_init)
    out = _attn_out_ffn_layer(h, a, attn_w_o[0].astype(BF16), row(norm_ffn_g[1]), ffn_w1[1].astype(BF16),
                              ffn_w3[1].astype(BF16), ffn_w2[1].astype(BF16), row(final_norm_g))
    return out.reshape(1, L, D_MODEL)
```

```python
import functools
import math

import jax
import jax.numpy as jnp
import numpy as np
from jax import lax
from jax.experimental import pallas as pl
from jax.experimental.pallas import tpu as pltpu

F32 = jnp.float32
BF16 = jnp.bfloat16

D_MODEL = 1024
CHUNK = 64
SSM_GROUP = 16
N_GROUPS = D_MODEL // SSM_GROUP
SSM_STATE = 64
N_HEADS = 8
HEAD_DIM = 64
V_DIM = 2 * HEAD_DIM
ROT_DIM = HEAD_DIM // 4
ROPE_THETA = 500000.0
FFN_HIDDEN = 2816
EPS = 1e-6
N_A_LAYERS = 1

LANES = 128
S5_CHUNK = 8
N_SLABS = D_MODEL // LANES
GROUPS_PER_SLAB = LANES // SSM_GROUP
SLAB_STATE = GROUPS_PER_SLAB * SSM_STATE
SLAB_W = S5_CHUNK * LANES
FFN_TILE = 256
Q_SCALE = HEAD_DIM ** -0.5 * math.log2(math.e)
VMEM_LIMIT = 56 * 1024 * 1024


def _cparams(sem):
    return pltpu.CompilerParams(dimension_semantics=sem, vmem_limit_bytes=VMEM_LIMIT)


def _rms(x, g):
    return x * lax.rsqrt(jnp.mean(x * x, axis=-1, keepdims=True) + EPS) * g


def _s5_weights(lam_re, lam_im, log_dt, b_re, b_im, c_re, c_im):
    hi = lax.Precision.HIGHEST
    lr, li = lam_re.astype(F32), lam_im.astype(F32)
    dt = jnp.exp(log_dt.astype(F32))[:, None]
    mag = jnp.exp(lr * dt)
    ar, ai = mag * jnp.cos(li * dt), mag * jnp.sin(li * dt)
    nr, ni = ar - 1.0, ai
    den = lr * lr + li * li
    fr = (nr * lr + ni * li) / den
    fi = (ni * lr - nr * li) / den
    br, bi = b_re.astype(F32), b_im.astype(F32)
    bbr = fr[..., None] * br - fi[..., None] * bi
    bbi = fr[..., None] * bi + fi[..., None] * br
    cr, ci = c_re.astype(F32), c_im.astype(F32)

    prs, pis = [jnp.ones_like(ar)], [jnp.zeros_like(ar)]
    for _ in range(S5_CHUNK):
        prs.append(prs[-1] * ar - pis[-1] * ai)
        pis.append(prs[-2] * ai + pis[-1] * ar)
    pr = jnp.stack(prs)
    pi = jnp.stack(pis)

    T = S5_CHUNK
    kr = pr[:T, :, :, None] * bbr[None] - pi[:T, :, :, None] * bbi[None]
    ki = pr[:T, :, :, None] * bbi[None] + pi[:T, :, :, None] * bbr[None]
    klag = (jnp.einsum('gdp,kgpc->kgcd', cr, kr, precision=hi)
            - jnp.einsum('gdp,kgpc->kgcd', ci, ki, precision=hi))
    toe = jnp.stack([jnp.concatenate([jnp.zeros_like(klag[:j]), klag[:T - j]], axis=0) for j in range(T)])
    win_r = kr[::-1]
    win_i = ki[::-1]
    p1r, p1i = pr[1:], pi[1:]
    out_r = cr[None] * p1r[:, :, None, :] - ci[None] * p1i[:, :, None, :]
    out_i = -cr[None] * p1i[:, :, None, :] - ci[None] * p1r[:, :, None, :]

    S, Gs = N_SLABS, GROUPS_PER_SLAB
    c_toe = toe.reshape(T, T, S, Gs, SSM_GROUP, SSM_GROUP).transpose(2, 0, 3, 4, 1, 5).reshape(S, SLAB_W, LANES)
    to_rows = lambda w: w.reshape(T, S, Gs, SSM_STATE, SSM_GROUP).transpose(1, 0, 2, 4, 3).reshape(S, SLAB_W, SSM_STATE)
    cw1 = jnp.concatenate([c_toe, to_rows(win_r), to_rows(win_i)], axis=-1).astype(BF16)
    to_out = lambda w: w.reshape(T, S, Gs, SSM_GROUP, SSM_STATE).transpose(1, 2, 4, 0, 3).reshape(S, SLAB_STATE, LANES)
    cwout = jnp.concatenate([to_out(out_r), to_out(out_i)], axis=1).astype(BF16)
    a_chunk = (pr[T].reshape(S, SLAB_STATE), pi[T].reshape(S, SLAB_STATE))
    return cw1, cwout, a_chunk


def _expansion_matrix():
    e = np.zeros((2 * LANES, SLAB_W + 2 * SLAB_STATE), np.float32)
    for h in range(GROUPS_PER_SLAB):
        for t in range(S5_CHUNK):
            for d in range(SSM_GROUP):
                e[t * SSM_GROUP + d, t * LANES + h * SSM_GROUP + d] = 1.0
        for p in range(SSM_STATE):
            e[LANES + p, SLAB_W + h * SSM_STATE + p] = 1.0
            e[LANES + SSM_STATE + p, SLAB_W + SLAB_STATE + h * SSM_STATE + p] = 1.0
    return jnp.asarray(e, BF16)


def _scan_powers(a_chunk):
    ar, ai = a_chunk
    r, i = ar, ai
    steps, tile_r, tile_i = [], [], []
    in_tile = jnp.arange(8)[None, :, None]
    for s in range(3):
        on = in_tile >= (1 << s)
        steps.append(jnp.stack([jnp.where(on, r[:, None, :], 0.0), jnp.where(on, i[:, None, :], 0.0)], axis=1))
        r, i = r * r - i * i, 2.0 * r * i
    r, i = ar, ai
    for _ in range(8):
        tile_r.append(r)
        tile_i.append(i)
        r, i = r * ar - i * ai, r * ai + i * ar
    return jnp.stack(steps, axis=1), jnp.stack([jnp.stack(tile_r, axis=1), jnp.stack(tile_i, axis=1)], axis=1)


def _s5_pre_kernel(x_ref, g_ref, u_ref):
    nb = u_ref.shape[1]
    u = _rms(x_ref[...], g_ref[...])
    uj = jnp.swapaxes(u.reshape(nb, S5_CHUNK, D_MODEL), 0, 1)
    for j in range(S5_CHUNK):
        for s in range(N_SLABS):
            u_ref[s, :, j * LANES:(j + 1) * LANES] = uj[j, :, s * LANES:(s + 1) * LANES].astype(BF16)


def _expand_block_diag(c, e_ref, w_ref, row_period, row_group):
    n_rows = c.shape[0]
    tile = SLAB_STATE
    for ct in range(w_ref.shape[1] // tile):
        x = jnp.dot(c, e_ref[:c.shape[1], ct * tile:(ct + 1) * tile], preferred_element_type=F32)
        row = lax.broadcasted_iota(jnp.int32, (n_rows, tile), 0)
        col = lax.broadcasted_iota(jnp.int32, (n_rows, tile), 1)
        if ct * tile < SLAB_W:
            col_g = (col % LANES) // SSM_GROUP
        else:
            col_g = col // SSM_STATE
        keep = (row % row_period) // row_group == col_g
        w_ref[:, ct * tile:(ct + 1) * tile] = jnp.where(keep, x, 0.0).astype(w_ref.dtype)


def _s5_core_kernel(u_ref, cw1_ref, cwout_ref, e_ref, stp_ref, ptab_ref, y_ref, w1_ref, wout_ref, ra_ref, rb_ref,
                    *, nb):
    _expand_block_diag(cw1_ref[0], e_ref, w1_ref, LANES, SSM_GROUP)
    _expand_block_diag(cwout_ref[0], e_ref, wout_ref, SLAB_STATE, SSM_STATE)
    n_blocks = u_ref.shape[1] // nb
    n_tiles = nb // 8
    in_tile = lax.broadcasted_iota(jnp.int32, (8, SLAB_STATE), 0)
    pt_r, pt_i = ptab_ref[0, 0], ptab_ref[0, 1]

    def inject(b, r_ref):
        r_ref[...] = jnp.dot(u_ref[0, b * nb:(b + 1) * nb, :], w1_ref[...], preferred_element_type=F32)

    inject(0, ra_ref)
    c_r = jnp.zeros((1, SLAB_STATE), F32)
    c_i = jnp.zeros((1, SLAB_STATE), F32)
    for b in range(n_blocks):
        cur, nxt = (ra_ref, rb_ref) if b % 2 == 0 else (rb_ref, ra_ref)
        if b + 1 < n_blocks:
            inject(b + 1, nxt)
        er = cur[:, SLAB_W:SLAB_W + SLAB_STATE].reshape(n_tiles, 8, SLAB_STATE)
        ei = cur[:, SLAB_W + SLAB_STATE:].reshape(n_tiles, 8, SLAB_STATE)
        for i in range(3):
            p_r, p_i = stp_ref[0, i, 0], stp_ref[0, i, 1]
            sr, si = pltpu.roll(er, 1 << i, 1), pltpu.roll(ei, 1 << i, 1)
            er, ei = er + p_r * sr - p_i * si, ei + p_r * si + p_i * sr
        pr, pi = [], []
        for k in range(n_tiles):
            b_r = jnp.broadcast_to(c_r, (8, SLAB_STATE))
            b_i = jnp.broadcast_to(c_i, (8, SLAB_STATE))
            t_r = er[k] + pt_r * b_r - pt_i * b_i
            t_i = ei[k] + pt_r * b_i + pt_i * b_r
            pr.append(jnp.where(in_tile == 0, b_r, pltpu.roll(t_r, 1, 0)))
            pi.append(jnp.where(in_tile == 0, b_i, pltpu.roll(t_i, 1, 0)))
            c_r, c_i = t_r[7:8], t_i[7:8]
        prev = jnp.concatenate([jnp.concatenate(pr, axis=0), jnp.concatenate(pi, axis=0)], axis=1).astype(BF16)
        y_ref[0, b * nb:(b + 1) * nb, :] = cur[:, :SLAB_W] + jnp.dot(prev, wout_ref[...],
                                                                      preferred_element_type=F32)


def _s5_post_kernel(x_ref, y_ref, g_ref, d_ref, wglu_ref, o_ref):
    nb = y_ref.shape[1]
    yj = jnp.stack([jnp.concatenate([y_ref[s, :, j * LANES:(j + 1) * LANES] for s in range(N_SLABS)], axis=1)
                    for j in range(S5_CHUNK)], axis=0)
    y = jnp.swapaxes(yj, 0, 1).reshape(S5_CHUNK * nb, D_MODEL)
    x = x_ref[...]
    z = jax.nn.gelu(y + d_ref[...] * _rms(x, g_ref[...]), approximate=True).astype(BF16)
    zz = jnp.dot(z, wglu_ref[...], preferred_element_type=F32)
    za, zb = zz[:, :D_MODEL], zz[:, D_MODEL:]
    o_ref[...] = x + za * (1.0 / (1.0 + jnp.exp(-zb)))


def _s5_layer(x, g_mix, cw1, cwout, stp, ptab, d_skip, w_glu):
    n_rows = x.shape[0] // S5_CHUNK
    nb = min(128, n_rows)
    nbp = min(64, n_rows)
    nbc = min(256, n_rows)
    assert n_rows % nb == 0 and n_rows % nbc == 0 and nbp % 8 == 0

    u = pl.pallas_call(
        _s5_pre_kernel,
        grid=(n_rows // nb,),
        in_specs=[pl.BlockSpec((S5_CHUNK * nb, D_MODEL), lambda i: (i, 0)),
                  pl.BlockSpec((1, D_MODEL), lambda i: (0, 0))],
        out_specs=pl.BlockSpec((N_SLABS, nb, SLAB_W), lambda i: (0, i, 0)),
        out_shape=jax.ShapeDtypeStruct((N_SLABS, n_rows, SLAB_W), BF16),
        compiler_params=_cparams(("parallel",)),
        name="s5_pre",
    )(x, g_mix)

    y = pl.pallas_call(
        functools.partial(_s5_core_kernel, nb=nbc),
        grid=(N_SLABS,),
        in_specs=[pl.BlockSpec((1, n_rows, SLAB_W), lambda s: (s, 0, 0)),
                  pl.BlockSpec((1, SLAB_W, 2 * LANES), lambda s: (s, 0, 0)),
                  pl.BlockSpec((1, 2 * SLAB_STATE, LANES), lambda s: (s, 0, 0)),
                  pl.BlockSpec((2 * LANES, SLAB_W + 2 * SLAB_STATE), lambda s: (0, 0)),
                  pl.BlockSpec((1,) + stp.shape[1:], lambda s: (s, 0, 0, 0, 0)),
                  pl.BlockSpec((1,) + ptab.shape[1:], lambda s: (s, 0, 0, 0))],
        out_specs=pl.BlockSpec((1, n_rows, SLAB_W), lambda s: (s, 0, 0)),
        out_shape=jax.ShapeDtypeStruct((N_SLABS, n_rows, SLAB_W), F32),
        scratch_shapes=[pltpu.VMEM((SLAB_W, SLAB_W + 2 * SLAB_STATE), BF16),
                        pltpu.VMEM((2 * SLAB_STATE, SLAB_W), BF16),
                        pltpu.VMEM((nbc, SLAB_W + 2 * SLAB_STATE), F32),
                        pltpu.VMEM((nbc, SLAB_W + 2 * SLAB_STATE), F32)],
        compiler_params=_cparams(("parallel",)),
        name="s5_core",
    )(u, cw1, cwout, _expansion_matrix(), stp, ptab)

    return pl.pallas_call(
        _s5_post_kernel,
        grid=(n_rows // nbp,),
        in_specs=[pl.BlockSpec((S5_CHUNK * nbp, D_MODEL), lambda i: (i, 0)),
                  pl.BlockSpec((N_SLABS, nbp, SLAB_W), lambda i: (0, i, 0)),
                  _const_spec((1, D_MODEL)),
                  _const_spec((1, D_MODEL)),
                  _const_spec((D_MODEL, 2 * D_MODEL))],
        out_specs=pl.BlockSpec((S5_CHUNK * nbp, D_MODEL), lambda i: (i, 0)),
        out_shape=jax.ShapeDtypeStruct(x.shape, F32),
        compiler_params=_cparams(("parallel",)),
        name="s5_post",
    )(x, y, g_mix, d_skip, w_glu)


def _ffn_body(h, g, w1_ref, w3_ref, w2_ref):
    hn = _rms(h, g).astype(BF16)
    acc = jnp.zeros(h.shape, F32)
    for f in range(FFN_HIDDEN // FFN_TILE):
        sl = slice(f * FFN_TILE, (f + 1) * FFN_TILE)
        a = jnp.dot(hn, w1_ref[:, sl], preferred_element_type=F32)
        b = jnp.dot(hn, w3_ref[:, sl], preferred_element_type=F32)
        t = (a * (1.0 / (1.0 + jnp.exp(-a))) * b).astype(BF16)
        acc = acc + jnp.dot(t, w2_ref[sl, :], preferred_element_type=F32)
    return h + acc


def _ffn_kernel(h_ref, g_ref, w1_ref, w3_ref, w2_ref, o_ref):
    o_ref[...] = _ffn_body(h_ref[...], g_ref[...], w1_ref, w3_ref, w2_ref)


def _attn_out_ffn_kernel(h_ref, a_ref, wo_ref, g_ref, w1_ref, w3_ref, w2_ref, gf_ref, o_ref):
    h = h_ref[...] + jnp.dot(a_ref[...], wo_ref[...], preferred_element_type=F32)
    o_ref[...] = _rms(_ffn_body(h, g_ref[...], w1_ref, w3_ref, w2_ref), gf_ref[...])


def _const_spec(shape):
    return pl.BlockSpec(shape, lambda i: (0,) * len(shape), pipeline_mode=pl.Buffered(1))


def _ffn_layer(h, g, w1, w3, w2):
    n, tm = h.shape[0], min(512, h.shape[0])
    return pl.pallas_call(
        _ffn_kernel,
        grid=(n // tm,),
        in_specs=[pl.BlockSpec((tm, D_MODEL), lambda i: (i, 0)),
                  _const_spec((1, D_MODEL)),
                  _const_spec((D_MODEL, FFN_HIDDEN)),
                  _const_spec((D_MODEL, FFN_HIDDEN)),
                  _const_spec((FFN_HIDDEN, D_MODEL))],
        out_specs=pl.BlockSpec((tm, D_MODEL), lambda i: (i, 0)),
        out_shape=jax.ShapeDtypeStruct(h.shape, F32),
        compiler_params=_cparams(("parallel",)),
        name="ffn",
    )(h, g, w1, w3, w2)


def _attn_out_ffn_layer(h, a, wo, g, w1, w3, w2, gf):
    n, tm = h.shape[0], min(512, h.shape[0])
    return pl.pallas_call(
        _attn_out_ffn_kernel,
        grid=(n // tm,),
        in_specs=[pl.BlockSpec((tm, D_MODEL), lambda i: (i, 0)),
                  pl.BlockSpec((tm, D_MODEL), lambda i: (i, 0)),
                  _const_spec((D_MODEL, D_MODEL)),
                  _const_spec((1, D_MODEL)),
                  _const_spec((D_MODEL, FFN_HIDDEN)),
                  _const_spec((D_MODEL, FFN_HIDDEN)),
                  _const_spec((FFN_HIDDEN, D_MODEL)),
                  _const_spec((1, D_MODEL))],
        out_specs=pl.BlockSpec((tm, D_MODEL), lambda i: (i, 0)),
        out_shape=jax.ShapeDtypeStruct(h.shape, F32),
        compiler_params=_cparams(("parallel",)),
        name="attn_out_ffn",
    )(h, a, wo, g, w1, w3, w2, gf)


def _rope(t, cos, sin_lo, sin_hi):
    return t * cos + pltpu.roll(t, LANES - ROT_DIM // 2, 1) * sin_lo + pltpu.roll(t, ROT_DIM // 2, 1) * sin_hi


def _pair_heads(x, hd, lane):
    hk = N_HEADS * HEAD_DIM
    c0 = (hd // 2) * LANES
    t1, t2 = x[:, c0:c0 + LANES], x[:, hk + c0:hk + c0 + LANES]
    if hd % 2 == 0:
        return jnp.where(lane < HEAD_DIM, t1, pltpu.roll(t2, HEAD_DIM, 1))
    return jnp.where(lane < HEAD_DIM, pltpu.roll(t1, HEAD_DIM, 1), t2)


def _proj_kernel(h_ref, gkv_ref, gq_ref, wkv_ref, wq_ref, cs_ref, sel_ref, one_ref, kk_ref, v_ref, qq_ref):
    h = h_ref[...]
    tab = jnp.dot(cs_ref[...], sel_ref[...], preferred_element_type=F32)
    cos, slo, shi = tab[:, :LANES] + one_ref[...], tab[:, LANES:2 * LANES], tab[:, 2 * LANES:]
    kv = jnp.dot(_rms(h, gkv_ref[...]).astype(BF16), wkv_ref[...], preferred_element_type=F32)
    q = jnp.dot(_rms(h, gq_ref[...]).astype(BF16), wq_ref[...], preferred_element_type=F32)
    lane = lax.broadcasted_iota(jnp.int32, (h.shape[0], LANES), 1)
    for hd in range(N_HEADS):
        kk_ref[hd] = _rope(_pair_heads(kv, hd, lane), cos, slo, shi).astype(BF16)
        qq_ref[hd] = (_rope(_pair_heads(q, hd, lane), cos, slo, shi) * Q_SCALE).astype(BF16)
        v_ref[hd] = kv[:, D_MODEL + hd * LANES:D_MODEL + (hd + 1) * LANES].astype(BF16)


def _proj_layer(h, gkv, gq, wkv, wq, cs, sel, one):
    n, tm = h.shape[0], min(512, h.shape[0])
    hshape = jax.ShapeDtypeStruct((N_HEADS, n, LANES), BF16)
    hspec = pl.BlockSpec((N_HEADS, tm, LANES), lambda i: (0, i, 0))
    return pl.pallas_call(
        _proj_kernel,
        grid=(n // tm,),
        in_specs=[pl.BlockSpec((tm, D_MODEL), lambda i: (i, 0)),
                  _const_spec((1, D_MODEL)), _const_spec((1, D_MODEL)),
                  _const_spec((D_MODEL, 2 * D_MODEL)), _const_spec((D_MODEL, D_MODEL)),
                  pl.BlockSpec((tm, cs.shape[1]), lambda i: (i, 0)),
                  _const_spec(sel.shape), _const_spec((1, LANES))],
        out_specs=[hspec, hspec, hspec],
        out_shape=[hshape, hshape, hshape],
        compiler_params=_cparams(("parallel",)),
        name="qkv_proj",
    )(h, gkv, gq, wkv, wq, cs, sel, one)


def _rope_tables(n):
    half = ROT_DIM // 2
    inv_freq = ROPE_THETA ** (-jnp.arange(half, dtype=F32) * 2.0 / ROT_DIM)
    ang = inv_freq[:, None] * jnp.arange(n, dtype=jnp.int32).astype(F32)[None, :]
    cs = jnp.concatenate([jnp.cos(ang), jnp.sin(ang)], axis=0).T
    hi = cs.astype(BF16)
    lo = (cs - hi.astype(F32)).astype(BF16)
    sel = np.zeros((ROT_DIM, 3 * LANES), np.float32)
    one = np.ones((1, LANES), np.float32)
    for l in range(LANES):
        d = l % HEAD_DIM
        if d < ROT_DIM:
            one[0, l] = 0.0
            sel[d % half, l] = 1.0
            if d < half:
                sel[half + d, LANES + l] = -1.0
            else:
                sel[half + d - half, 2 * LANES + l] = 1.0
    sel = np.concatenate([sel, sel], axis=0)
    return jnp.concatenate([hi, lo], axis=1), jnp.asarray(sel, BF16), jnp.asarray(one)


def _attn_kernel(lam_ref, qq_ref, kk_ref, v_ref, g_ref, o_ref, q2_ref, sa_ref, sb_ref, ma_ref, mb_ref,
                 ala_ref, alb_ref, mrun_ref, acco_ref, accl_ref, *, tq, tk, lam_init):
    qi = pl.program_id(1)
    lane = lax.broadcasted_iota(jnp.int32, (tk, LANES), 1)
    zero = jnp.zeros((tk, LANES), BF16)
    for half in range(2):
        q = qq_ref[0, half * tk:(half + 1) * tk, :]
        q2_ref[(2 * half) * tk:(2 * half + 1) * tk, :] = jnp.where(lane < HEAD_DIM, q, zero)
        q2_ref[(2 * half + 1) * tk:(2 * half + 2) * tk, :] = jnp.where(lane >= HEAD_DIM, q, zero)
    mrun_ref[...] = jnp.full(mrun_ref.shape, -jnp.inf, F32)
    acco_ref[...] = jnp.zeros(acco_ref.shape, F32)
    accl_ref[...] = jnp.zeros(accl_ref.shape, F32)
    buf_a = (sa_ref, ma_ref, ala_ref)
    buf_b = (sb_ref, mb_ref, alb_ref)
    late = slice(2 * tk, 4 * tk)

    def causal(s):
        r = lax.broadcasted_iota(jnp.int32, s.shape, 0)
        c = lax.broadcasted_iota(jnp.int32, s.shape, 1)
        return jnp.where((c // CHUNK) <= ((r % tk) // CHUNK), s, -jnp.inf)

    def stats(s, rows, buf):
        s_ref, m_ref, al_ref = buf
        s_ref[rows, :] = s
        m_prev = mrun_ref[rows, :]
        m_new = jnp.maximum(m_prev, jnp.max(s, axis=1, keepdims=True))
        al_ref[rows, :] = jnp.exp2(m_prev - m_new)
        m_ref[rows, :] = m_new
        mrun_ref[rows, :] = m_new

    def scores(blk, buf, kind):
        k = kk_ref[0, pl.ds(pl.multiple_of(blk * tk, tk), tk), :]
        dims = (((1,), (1,)), ((), ()))
        if kind == "last":
            stats(causal(lax.dot_general(q2_ref[late, :], k, dims, preferred_element_type=F32)), late, buf)
            return
        s = lax.dot_general(q2_ref[...], k, dims, preferred_element_type=F32)
        if kind == "diag":
            stats(causal(s[:2 * tk]), slice(0, 2 * tk), buf)
            stats(s[2 * tk:], late, buf)
        else:
            stats(s, slice(None), buf)

    def accumulate(blk, buf, rows=slice(None)):
        s_ref, m_ref, al_ref = buf
        v = v_ref[0, pl.ds(pl.multiple_of(blk * tk, tk), tk), :]
        v1 = jnp.concatenate([v, jnp.ones_like(v)], axis=1)
        m = m_ref[rows, :]
        p = jnp.concatenate([jnp.exp2(s_ref[rows, j * LANES:(j + 1) * LANES] - m) for j in range(tk // LANES)],
                            axis=1).astype(BF16)
        pv = jnp.dot(p, v1, preferred_element_type=F32)
        al = al_ref[rows, :]
        acco_ref[rows, :] = al * acco_ref[rows, :] + pv[:, :LANES]
        accl_ref[rows, :] = al * accl_ref[rows, :] + pv[:, LANES:]

    @pl.when(qi == 0)
    def _():
        scores(0, buf_a, "diag")

    @pl.when(qi > 0)
    def _():
        scores(0, buf_a, "full")

    def pair(t, next_kind):
        scores(2 * t + 1, buf_b, "full")
        accumulate(2 * t, buf_a)
        scores(2 * t + 2, buf_a, next_kind)
        accumulate(2 * t + 1, buf_b)

    n_loop = jnp.maximum(qi - 1, 0)
    odd = n_loop % 2

    @pl.when(odd == 1)
    def _():
        pair(0, "full")

    def body(u, c):
        t = odd + 2 * u
        pair(t, "full")
        pair(t + 1, "full")
        return c

    lax.fori_loop(0, n_loop // 2, body, 0)

    @pl.when(qi > 0)
    def _():
        pair(qi - 1, "diag")

    scores(2 * qi + 1, buf_b, "last")
    accumulate(2 * qi, buf_a)
    accumulate(2 * qi + 1, buf_b, late)

    o = acco_ref[...] / accl_ref[...]
    lam = lam_ref[0]
    o = jnp.concatenate([o[:tk] - lam * o[tk:2 * tk], o[2 * tk:3 * tk] - lam * o[3 * tk:]], axis=0)
    o_ref[...] = (_rms(o, g_ref[...]) * (1.0 - lam_init)).astype(o_ref.dtype)


def _attn_layer(lam, qq, kk, v, g, lam_init):
    n = qq.shape[1]
    tq = min(1024, n)
    tk = tq // 2
    assert n % tq == 0 and tk % CHUNK == 0 and tk % LANES == 0
    stat = pltpu.VMEM((2 * tq, LANES), F32)
    return pl.pallas_call(
        functools.partial(_attn_kernel, tq=tq, tk=tk, lam_init=lam_init),
        grid=(N_HEADS, n // tq),
        in_specs=[pl.BlockSpec(memory_space=pltpu.SMEM),
                  pl.BlockSpec((1, tq, LANES), lambda h, i: (h, i, 0)),
                  pl.BlockSpec((1, n, LANES), lambda h, i: (h, 0, 0)),
                  pl.BlockSpec((1, n, LANES), lambda h, i: (h, 0, 0)),
                  pl.BlockSpec((1, LANES), lambda h, i: (0, 0))],
        out_specs=pl.BlockSpec((tq, LANES), lambda h, i: (i, h)),
        out_shape=jax.ShapeDtypeStruct((n, N_HEADS * V_DIM), BF16),
        scratch_shapes=[pltpu.VMEM((2 * tq, LANES), BF16),
                        pltpu.VMEM((2 * tq, tk), F32), pltpu.VMEM((2 * tq, tk), F32),
                        stat, stat, stat, stat, stat, stat, stat],
        compiler_params=_cparams(("parallel", "arbitrary")),
        name="diff_attn",
    )(lam, qq, kk, v, g)


def kernel(x, norm_mix_g, norm_ffn_g, ffn_w1, ffn_w3, ffn_w2, ssm_lam_re, ssm_lam_im, ssm_log_dt, ssm_b_re, ssm_b_im, ssm_c_re, ssm_c_im, ssm_d, ssm_w_glu, kv_norm_g, w_kv, attn_w_q, attn_lq1, attn_lk1, attn_lq2, attn_lk2, attn_subln_g, attn_w_o, final_norm_g):
    bsz, L, _ = x.shape
    assert bsz == 1 and L % (S5_CHUNK * 8) == 0
    row = lambda t: t.reshape(1, -1).astype(F32)

    cw1, cwout, a_chunk = _s5_weights(ssm_lam_re[0], ssm_lam_im[0], ssm_log_dt[0], ssm_b_re[0], ssm_b_im[0],
                                      ssm_c_re[0], ssm_c_im[0])
    stp, ptab = _scan_powers(a_chunk)
    h = _s5_layer(x.reshape(L, D_MODEL), row(norm_mix_g[0]), cw1, cwout, stp, ptab, row(ssm_d[0]),
                  ssm_w_glu[0].astype(BF16))
    h = _ffn_layer(h, row(norm_ffn_g[0]), ffn_w1[0].astype(BF16), ffn_w3[0].astype(BF16), ffn_w2[0].astype(BF16))

    hk = N_HEADS * HEAD_DIM
    cs, sel, one = _rope_tables(L)
    kk, v, qq = _proj_layer(h, row(kv_norm_g), row(norm_mix_g[1]), w_kv.astype(BF16), attn_w_q[0].astype(BF16),
                            cs, sel, one)
    lam_init = 0.8 - 0.6 * math.exp(-0.3 * N_A_LAYERS)
    lam = (jnp.exp(jnp.sum(attn_lq1[0].astype(F32) * attn_lk1[0].astype(F32)))
           - jnp.exp(jnp.sum(attn_lq2[0].astype(F32) * attn_lk2[0].astype(F32))) + lam_init).reshape(1)
    a = _attn_layer(lam, qq, kk, v, row(attn_subln_g[0]), lam_init)
    out = _attn_out_ffn_layer(h, a, attn_w_o[0].astype(BF16), row(norm_ffn_g[1]), ffn_w1[1].astype(BF16),
                              ffn_w3[1].astype(BF16), ffn_w2[1].astype(BF16), row(final_norm_g))
    return out.reshape(1, L, D_MODEL)
```

```python
import functools
import math

import jax
import jax.numpy as jnp
import numpy as np
from jax import lax
from jax.experimental import pallas as pl
from jax.experimental.pallas import tpu as pltpu

F32 = jnp.float32
BF16 = jnp.bfloat16

D_MODEL = 1024
CHUNK = 64
SSM_GROUP = 16
N_GROUPS = D_MODEL // SSM_GROUP
SSM_STATE = 64
N_HEADS = 8
HEAD_DIM = 64
V_DIM = 2 * HEAD_DIM
ROT_DIM = HEAD_DIM // 4
ROPE_THETA = 500000.0
FFN_HIDDEN = 2816
EPS = 1e-6
N_A_LAYERS = 1

LANES = 128
S5_CHUNK = 8
N_SLABS = D_MODEL // LANES
GROUPS_PER_SLAB = LANES // SSM_GROUP
SLAB_STATE = GROUPS_PER_SLAB * SSM_STATE
SLAB_W = S5_CHUNK * LANES
FFN_TILE = 256
Q_SCALE = HEAD_DIM ** -0.5 * math.log2(math.e)
VMEM_LIMIT = 56 * 1024 * 1024


def _cparams(sem):
    return pltpu.CompilerParams(dimension_semantics=sem, vmem_limit_bytes=VMEM_LIMIT)


def _rms(x, g):
    return x * lax.rsqrt(jnp.mean(x * x, axis=-1, keepdims=True) + EPS) * g


def _s5_weights(lam_re, lam_im, log_dt, b_re, b_im, c_re, c_im):
    hi = lax.Precision.HIGHEST
    lr, li = lam_re.astype(F32), lam_im.astype(F32)
    dt = jnp.exp(log_dt.astype(F32))[:, None]
    mag = jnp.exp(lr * dt)
    ar, ai = mag * jnp.cos(li * dt), mag * jnp.sin(li * dt)
    nr, ni = ar - 1.0, ai
    den = lr * lr + li * li
    fr = (nr * lr + ni * li) / den
    fi = (ni * lr - nr * li) / den
    br, bi = b_re.astype(F32), b_im.astype(F32)
    bbr = fr[..., None] * br - fi[..., None] * bi
    bbi = fr[..., None] * bi + fi[..., None] * br
    cr, ci = c_re.astype(F32), c_im.astype(F32)

    prs, pis = [jnp.ones_like(ar)], [jnp.zeros_like(ar)]
    for _ in range(S5_CHUNK):
        prs.append(prs[-1] * ar - pis[-1] * ai)
        pis.append(prs[-2] * ai + pis[-1] * ar)
    pr = jnp.stack(prs)
    pi = jnp.stack(pis)

    T = S5_CHUNK
    kr = pr[:T, :, :, None] * bbr[None] - pi[:T, :, :, None] * bbi[None]
    ki = pr[:T, :, :, None] * bbi[None] + pi[:T, :, :, None] * bbr[None]
    klag = (jnp.einsum('gdp,kgpc->kgcd', cr, kr, precision=hi)
            - jnp.einsum('gdp,kgpc->kgcd', ci, ki, precision=hi))
    toe = jnp.stack([jnp.concatenate([jnp.zeros_like(klag[:j]), klag[:T - j]], axis=0) for j in range(T)])
    win_r = kr[::-1]
    win_i = ki[::-1]
    p1r, p1i = pr[1:], pi[1:]
    out_r = cr[None] * p1r[:, :, None, :] - ci[None] * p1i[:, :, None, :]
    out_i = -cr[None] * p1i[:, :, None, :] - ci[None] * p1r[:, :, None, :]

    S, Gs = N_SLABS, GROUPS_PER_SLAB
    c_toe = toe.reshape(T, T, S, Gs, SSM_GROUP, SSM_GROUP).transpose(2, 0, 3, 4, 1, 5).reshape(S, SLAB_W, LANES)
    to_rows = lambda w: w.reshape(T, S, Gs, SSM_STATE, SSM_GROUP).transpose(1, 0, 2, 4, 3).reshape(S, SLAB_W, SSM_STATE)
    cw1 = jnp.concatenate([c_toe, to_rows(win_r), to_rows(win_i)], axis=-1).astype(BF16)
    to_out = lambda w: w.reshape(T, S, Gs, SSM_GROUP, SSM_STATE).transpose(1, 2, 4, 0, 3).reshape(S, SLAB_STATE, LANES)
    cwout = jnp.concatenate([to_out(out_r), to_out(out_i)], axis=1).astype(BF16)
    a_chunk = (pr[T].reshape(S, SLAB_STATE), pi[T].reshape(S, SLAB_STATE))
    return cw1, cwout, a_chunk


def _expansion_matrix():
    e = np.zeros((2 * LANES, SLAB_W + 2 * SLAB_STATE), np.float32)
    for h in range(GROUPS_PER_SLAB):
        for t in range(S5_CHUNK):
            for d in range(SSM_GROUP):
                e[t * SSM_GROUP + d, t * LANES + h * SSM_GROUP + d] = 1.0
        for p in range(SSM_STATE):
            e[LANES + p, SLAB_W + h * SSM_STATE + p] = 1.0
            e[LANES + SSM_STATE + p, SLAB_W + SLAB_STATE + h * SSM_STATE + p] = 1.0
    return jnp.asarray(e, BF16)


def _scan_powers(a_chunk):
    ar, ai = a_chunk
    r, i = ar, ai
    steps, tile_r, tile_i = [], [], []
    in_tile = jnp.arange(8)[None, :, None]
    for s in range(3):
        on = in_tile >= (1 << s)
        steps.append(jnp.stack([jnp.where(on, r[:, None, :], 0.0), jnp.where(on, i[:, None, :], 0.0)], axis=1))
        r, i = r * r - i * i, 2.0 * r * i
    r, i = ar, ai
    for _ in range(8):
        tile_r.append(r)
        tile_i.append(i)
        r, i = r * ar - i * ai, r * ai + i * ar
    return jnp.stack(steps, axis=1), jnp.stack([jnp.stack(tile_r, axis=1), jnp.stack(tile_i, axis=1)], axis=1)


def _s5_pre_kernel(x_ref, g_ref, u_ref):
    nb = u_ref.shape[1]
    u = _rms(x_ref[...], g_ref[...])
    uj = jnp.swapaxes(u.reshape(nb, S5_CHUNK, D_MODEL), 0, 1)
    for j in range(S5_CHUNK):
        for s in range(N_SLABS):
            u_ref[s, :, j * LANES:(j + 1) * LANES] = uj[j, :, s * LANES:(s + 1) * LANES].astype(BF16)


def _expand_block_diag(c, e_ref, w_ref, row_period, row_group):
    n_rows = c.shape[0]
    tile = SLAB_STATE
    for ct in range(w_ref.shape[1] // tile):
        x = jnp.dot(c, e_ref[:c.shape[1], ct * tile:(ct + 1) * tile], preferred_element_type=F32)
        row = lax.broadcasted_iota(jnp.int32, (n_rows, tile), 0)
        col = lax.broadcasted_iota(jnp.int32, (n_rows, tile), 1)
        if ct * tile < SLAB_W:
            col_g = (col % LANES) // SSM_GROUP
        else:
            col_g = col // SSM_STATE
        keep = (row % row_period) // row_group == col_g
        w_ref[:, ct * tile:(ct + 1) * tile] = jnp.where(keep, x, 0.0).astype(w_ref.dtype)


def _s5_core_kernel(u_ref, cw1_ref, cwout_ref, e_ref, stp_ref, ptab_ref, y_ref, w1_ref, wout_ref, ra_ref, rb_ref,
                    *, nb):
    _expand_block_diag(cw1_ref[0], e_ref, w1_ref, LANES, SSM_GROUP)
    _expand_block_diag(cwout_ref[0], e_ref, wout_ref, SLAB_STATE, SSM_STATE)
    n_blocks = u_ref.shape[1] // nb
    n_tiles = nb // 8
    in_tile = lax.broadcasted_iota(jnp.int32, (8, SLAB_STATE), 0)
    pt_r, pt_i = ptab_ref[0, 0], ptab_ref[0, 1]

    def inject(b, r_ref):
        r_ref[...] = jnp.dot(u_ref[0, b * nb:(b + 1) * nb, :], w1_ref[...], preferred_element_type=F32)

    inject(0, ra_ref)
    c_r = jnp.zeros((1, SLAB_STATE), F32)
    c_i = jnp.zeros((1, SLAB_STATE), F32)
    for b in range(n_blocks):
        cur, nxt = (ra_ref, rb_ref) if b % 2 == 0 else (rb_ref, ra_ref)
        if b + 1 < n_blocks:
            inject(b + 1, nxt)
        er = cur[:, SLAB_W:SLAB_W + SLAB_STATE].reshape(n_tiles, 8, SLAB_STATE)
        ei = cur[:, SLAB_W + SLAB_STATE:].reshape(n_tiles, 8, SLAB_STATE)
        for i in range(3):
            p_r, p_i = stp_ref[0, i, 0], stp_ref[0, i, 1]
            sr, si = pltpu.roll(er, 1 << i, 1), pltpu.roll(ei, 1 << i, 1)
            er, ei = er + p_r * sr - p_i * si, ei + p_r * si + p_i * sr
        pr, pi = [], []
        for k in range(n_tiles):
            b_r = jnp.broadcast_to(c_r, (8, SLAB_STATE))
            b_i = jnp.broadcast_to(c_i, (8, SLAB_STATE))
            t_r = er[k] + pt_r * b_r - pt_i * b_i
            t_i = ei[k] + pt_r * b_i + pt_i * b_r
            pr.append(jnp.where(in_tile == 0, b_r, pltpu.roll(t_r, 1, 0)))
            pi.append(jnp.where(in_tile == 0, b_i, pltpu.roll(t_i, 1, 0)))
            c_r, c_i = t_r[7:8], t_i[7:8]
        prev = jnp.concatenate([jnp.concatenate(pr, axis=0), jnp.concatenate(pi, axis=0)], axis=1).astype(BF16)
        y_ref[0, b * nb:(b + 1) * nb, :] = cur[:, :SLAB_W] + jnp.dot(prev, wout_ref[...],
                                                                      preferred_element_type=F32)


def _s5_post_kernel(x_ref, y_ref, g_ref, d_ref, wglu_ref, o_ref):
    nb = y_ref.shape[1] // 2
    for half in range(2):
        rs = slice(half * nb, (half + 1) * nb)
        ps = slice(half * S5_CHUNK * nb, (half + 1) * S5_CHUNK * nb)
        yj = jnp.stack([jnp.concatenate([y_ref[s, rs, j * LANES:(j + 1) * LANES] for s in range(N_SLABS)], axis=1)
                        for j in range(S5_CHUNK)], axis=0)
        y = jnp.swapaxes(yj, 0, 1).reshape(S5_CHUNK * nb, D_MODEL)
        x = x_ref[ps, :]
        z = jax.nn.gelu(y + d_ref[...] * _rms(x, g_ref[...]), approximate=True).astype(BF16)
        zz = jnp.dot(z, wglu_ref[...], preferred_element_type=F32)
        za, zb = zz[:, :D_MODEL], zz[:, D_MODEL:]
        o_ref[ps, :] = x + za * (1.0 / (1.0 + jnp.exp(-zb)))


def _s5_layer(x, g_mix, cw1, cwout, stp, ptab, d_skip, w_glu):
    n_rows = x.shape[0] // S5_CHUNK
    nb = min(128, n_rows)
    nbp = min(128, n_rows)
    nbc = min(256, n_rows)
    assert n_rows % nb == 0 and n_rows % nbc == 0 and nbp % 16 == 0

    u = pl.pallas_call(
        _s5_pre_kernel,
        grid=(n_rows // nb,),
        in_specs=[pl.BlockSpec((S5_CHUNK * nb, D_MODEL), lambda i: (i, 0)),
                  pl.BlockSpec((1, D_MODEL), lambda i: (0, 0))],
        out_specs=pl.BlockSpec((N_SLABS, nb, SLAB_W), lambda i: (0, i, 0)),
        out_shape=jax.ShapeDtypeStruct((N_SLABS, n_rows, SLAB_W), BF16),
        compiler_params=_cparams(("parallel",)),
        name="s5_pre",
    )(x, g_mix)

    y = pl.pallas_call(
        functools.partial(_s5_core_kernel, nb=nbc),
        grid=(N_SLABS,),
        in_specs=[pl.BlockSpec((1, n_rows, SLAB_W), lambda s: (s, 0, 0)),
                  pl.BlockSpec((1, SLAB_W, 2 * LANES), lambda s: (s, 0, 0)),
                  pl.BlockSpec((1, 2 * SLAB_STATE, LANES), lambda s: (s, 0, 0)),
                  pl.BlockSpec((2 * LANES, SLAB_W + 2 * SLAB_STATE), lambda s: (0, 0)),
                  pl.BlockSpec((1,) + stp.shape[1:], lambda s: (s, 0, 0, 0, 0)),
                  pl.BlockSpec((1,) + ptab.shape[1:], lambda s: (s, 0, 0, 0))],
        out_specs=pl.BlockSpec((1, n_rows, SLAB_W), lambda s: (s, 0, 0)),
        out_shape=jax.ShapeDtypeStruct((N_SLABS, n_rows, SLAB_W), F32),
        scratch_shapes=[pltpu.VMEM((SLAB_W, SLAB_W + 2 * SLAB_STATE), BF16),
                        pltpu.VMEM((2 * SLAB_STATE, SLAB_W), BF16),
                        pltpu.VMEM((nbc, SLAB_W + 2 * SLAB_STATE), F32),
                        pltpu.VMEM((nbc, SLAB_W + 2 * SLAB_STATE), F32)],
        compiler_params=_cparams(("parallel",)),
        name="s5_core",
    )(u, cw1, cwout, _expansion_matrix(), stp, ptab)

    return pl.pallas_call(
        _s5_post_kernel,
        grid=(n_rows // nbp,),
        in_specs=[pl.BlockSpec((S5_CHUNK * nbp, D_MODEL), lambda i: (i, 0)),
                  pl.BlockSpec((N_SLABS, nbp, SLAB_W), lambda i: (0, i, 0)),
                  _const_spec((1, D_MODEL)),
                  _const_spec((1, D_MODEL)),
                  _const_spec((D_MODEL, 2 * D_MODEL))],
        out_specs=pl.BlockSpec((S5_CHUNK * nbp, D_MODEL), lambda i: (i, 0)),
        out_shape=jax.ShapeDtypeStruct(x.shape, F32),
        compiler_params=_cparams(("parallel",)),
        name="s5_post",
    )(x, y, g_mix, d_skip, w_glu)


def _ffn_body(h, g, w1_ref, w3_ref, w2_ref):
    hn = _rms(h, g).astype(BF16)
    acc = jnp.zeros(h.shape, F32)
    for f in range(FFN_HIDDEN // FFN_TILE):
        sl = slice(f * FFN_TILE, (f + 1) * FFN_TILE)
        a = jnp.dot(hn, w1_ref[:, sl], preferred_element_type=F32)
        b = jnp.dot(hn, w3_ref[:, sl], preferred_element_type=F32)
        t = (a * (1.0 / (1.0 + jnp.exp(-a))) * b).astype(BF16)
        acc = acc + jnp.dot(t, w2_ref[sl, :], preferred_element_type=F32)
    return h + acc


def _ffn_kernel(h_ref, g_ref, w1_ref, w3_ref, w2_ref, o_ref):
    o_ref[...] = _ffn_body(h_ref[...], g_ref[...], w1_ref, w3_ref, w2_ref)


def _attn_out_ffn_kernel(h_ref, a_ref, wo_ref, g_ref, w1_ref, w3_ref, w2_ref, gf_ref, o_ref):
    h = h_ref[...] + jnp.dot(a_ref[...], wo_ref[...], preferred_element_type=F32)
    o_ref[...] = _rms(_ffn_body(h, g_ref[...], w1_ref, w3_ref, w2_ref), gf_ref[...])


def _const_spec(shape):
    return pl.BlockSpec(shape, lambda i: (0,) * len(shape), pipeline_mode=pl.Buffered(1))


def _ffn_layer(h, g, w1, w3, w2):
    n, tm = h.shape[0], min(512, h.shape[0])
    return pl.pallas_call(
        _ffn_kernel,
        grid=(n // tm,),
        in_specs=[pl.BlockSpec((tm, D_MODEL), lambda i: (i, 0)),
                  _const_spec((1, D_MODEL)),
                  _const_spec((D_MODEL, FFN_HIDDEN)),
                  _const_spec((D_MODEL, FFN_HIDDEN)),
                  _const_spec((FFN_HIDDEN, D_MODEL))],
        out_specs=pl.BlockSpec((tm, D_MODEL), lambda i: (i, 0)),
        out_shape=jax.ShapeDtypeStruct(h.shape, F32),
        compiler_params=_cparams(("parallel",)),
        name="ffn",
    )(h, g, w1, w3, w2)


def _attn_out_ffn_layer(h, a, wo, g, w1, w3, w2, gf):
    n, tm = h.shape[0], min(512, h.shape[0])
    return pl.pallas_call(
        _attn_out_ffn_kernel,
        grid=(n // tm,),
        in_specs=[pl.BlockSpec((tm, D_MODEL), lambda i: (i, 0)),
                  pl.BlockSpec((tm, D_MODEL), lambda i: (i, 0)),
                  _const_spec((D_MODEL, D_MODEL)),
                  _const_spec((1, D_MODEL)),
                  _const_spec((D_MODEL, FFN_HIDDEN)),
                  _const_spec((D_MODEL, FFN_HIDDEN)),
                  _const_spec((FFN_HIDDEN, D_MODEL)),
                  _const_spec((1, D_MODEL))],
        out_specs=pl.BlockSpec((tm, D_MODEL), lambda i: (i, 0)),
        out_shape=jax.ShapeDtypeStruct(h.shape, F32),
        compiler_params=_cparams(("parallel",)),
        name="attn_out_ffn",
    )(h, a, wo, g, w1, w3, w2, gf)


def _rope(t, cos, sin_lo, sin_hi):
    return t * cos + pltpu.roll(t, LANES - ROT_DIM // 2, 1) * sin_lo + pltpu.roll(t, ROT_DIM // 2, 1) * sin_hi


def _pair_heads(x, hd, lane):
    hk = N_HEADS * HEAD_DIM
    c0 = (hd // 2) * LANES
    t1, t2 = x[:, c0:c0 + LANES], x[:, hk + c0:hk + c0 + LANES]
    if hd % 2 == 0:
        return jnp.where(lane < HEAD_DIM, t1, pltpu.roll(t2, HEAD_DIM, 1))
    return jnp.where(lane < HEAD_DIM, pltpu.roll(t1, HEAD_DIM, 1), t2)


def _proj_kernel(h_ref, gkv_ref, gq_ref, wkv_ref, wq_ref, cs_ref, sel_ref, one_ref, kk_ref, v_ref, qq_ref):
    tm = h_ref.shape[0] // 2
    lane = lax.broadcasted_iota(jnp.int32, (tm, LANES), 1)
    for half in range(2):
        rs = slice(half * tm, (half + 1) * tm)
        h = h_ref[rs, :]
        tab = jnp.dot(cs_ref[rs, :], sel_ref[...], preferred_element_type=F32)
        cos, slo, shi = tab[:, :LANES] + one_ref[...], tab[:, LANES:2 * LANES], tab[:, 2 * LANES:]
        kv = jnp.dot(_rms(h, gkv_ref[...]).astype(BF16), wkv_ref[...], preferred_element_type=F32)
        q = jnp.dot(_rms(h, gq_ref[...]).astype(BF16), wq_ref[...], preferred_element_type=F32)
        for hd in range(N_HEADS):
            kk_ref[hd, rs, :] = _rope(_pair_heads(kv, hd, lane), cos, slo, shi).astype(BF16)
            qq_ref[hd, rs, :] = (_rope(_pair_heads(q, hd, lane), cos, slo, shi) * Q_SCALE).astype(BF16)
            v_ref[hd, rs, :] = kv[:, D_MODEL + hd * LANES:D_MODEL + (hd + 1) * LANES].astype(BF16)


def _proj_layer(h, gkv, gq, wkv, wq, cs, sel, one):
    n, tm = h.shape[0], min(1024, h.shape[0])
    hshape = jax.ShapeDtypeStruct((N_HEADS, n, LANES), BF16)
    hspec = pl.BlockSpec((N_HEADS, tm, LANES), lambda i: (0, i, 0))
    return pl.pallas_call(
        _proj_kernel,
        grid=(n // tm,),
        in_specs=[pl.BlockSpec((tm, D_MODEL), lambda i: (i, 0)),
                  _const_spec((1, D_MODEL)), _const_spec((1, D_MODEL)),
                  _const_spec((D_MODEL, 2 * D_MODEL)), _const_spec((D_MODEL, D_MODEL)),
                  pl.BlockSpec((tm, cs.shape[1]), lambda i: (i, 0)),
                  _const_spec(sel.shape), _const_spec((1, LANES))],
        out_specs=[hspec, hspec, hspec],
        out_shape=[hshape, hshape, hshape],
        compiler_params=_cparams(("parallel",)),
        name="qkv_proj",
    )(h, gkv, gq, wkv, wq, cs, sel, one)


def _rope_tables(n):
    half = ROT_DIM // 2
    inv_freq = ROPE_THETA ** (-jnp.arange(half, dtype=F32) * 2.0 / ROT_DIM)
    ang = inv_freq[:, None] * jnp.arange(n, dtype=jnp.int32).astype(F32)[None, :]
    cs = jnp.concatenate([jnp.cos(ang), jnp.sin(ang)], axis=0).T
    hi = cs.astype(BF16)
    lo = (cs - hi.astype(F32)).astype(BF16)
    sel = np.zeros((ROT_DIM, 3 * LANES), np.float32)
    one = np.ones((1, LANES), np.float32)
    for l in range(LANES):
        d = l % HEAD_DIM
        if d < ROT_DIM:
            one[0, l] = 0.0
            sel[d % half, l] = 1.0
            if d < half:
                sel[half + d, LANES + l] = -1.0
            else:
                sel[half + d - half, 2 * LANES + l] = 1.0
    sel = np.concatenate([sel, sel], axis=0)
    return jnp.concatenate([hi, lo], axis=1), jnp.asarray(sel, BF16), jnp.asarray(one)


def _attn_kernel(lam_ref, qq_ref, kk_ref, v_ref, g_ref, o_ref, q2_ref, sa_ref, sb_ref, ma_ref, mb_ref,
                 ala_ref, alb_ref, mrun_ref, acco_ref, accl_ref, *, tq, tk, lam_init):
    qi = pl.program_id(1)
    lane = lax.broadcasted_iota(jnp.int32, (tk, LANES), 1)
    zero = jnp.zeros((tk, LANES), BF16)
    for half in range(2):
        q = qq_ref[0, half * tk:(half + 1) * tk, :]
        q2_ref[(2 * half) * tk:(2 * half + 1) * tk, :] = jnp.where(lane < HEAD_DIM, q, zero)
        q2_ref[(2 * half + 1) * tk:(2 * half + 2) * tk, :] = jnp.where(lane >= HEAD_DIM, q, zero)
    mrun_ref[...] = jnp.full(mrun_ref.shape, -jnp.inf, F32)
    acco_ref[...] = jnp.zeros(acco_ref.shape, F32)
    accl_ref[...] = jnp.zeros(accl_ref.shape, F32)
    buf_a = (sa_ref, ma_ref, ala_ref)
    buf_b = (sb_ref, mb_ref, alb_ref)
    late = slice(2 * tk, 4 * tk)

    def causal(s):
        r = lax.broadcasted_iota(jnp.int32, s.shape, 0)
        c = lax.broadcasted_iota(jnp.int32, s.shape, 1)
        return jnp.where((c // CHUNK) <= ((r % tk) // CHUNK), s, -jnp.inf)

    def stats(s, rows, buf):
        s_ref, m_ref, al_ref = buf
        s_ref[rows, :] = s
        m_prev = mrun_ref[rows, :]
        m_new = jnp.maximum(m_prev, jnp.max(s, axis=1, keepdims=True))
        al_ref[rows, :] = jnp.exp2(m_prev - m_new)
        m_ref[rows, :] = m_new
        mrun_ref[rows, :] = m_new

    def scores(blk, buf, kind):
        k = kk_ref[0, pl.ds(pl.multiple_of(blk * tk, tk), tk), :]
        dims = (((1,), (1,)), ((), ()))
        if kind == "last":
            stats(causal(lax.dot_general(q2_ref[late, :], k, dims, preferred_element_type=F32)), late, buf)
            return
        s = lax.dot_general(q2_ref[...], k, dims, preferred_element_type=F32)
        if kind == "diag":
            stats(causal(s[:2 * tk]), slice(0, 2 * tk), buf)
            stats(s[2 * tk:], late, buf)
        else:
            stats(s, slice(None), buf)

    def accumulate(blk, buf, rows=slice(None)):
        s_ref, m_ref, al_ref = buf
        v = v_ref[0, pl.ds(pl.multiple_of(blk * tk, tk), tk), :]
        v1 = jnp.concatenate([v, jnp.ones_like(v)], axis=1)
        m = m_ref[rows, :]
        p = jnp.concatenate([jnp.exp2(s_ref[rows, j * LANES:(j + 1) * LANES] - m) for j in range(tk // LANES)],
                            axis=1).astype(BF16)
        pv = jnp.dot(p, v1, preferred_element_type=F32)
        al = al_ref[rows, :]
        acco_ref[rows, :] = al * acco_ref[rows, :] + pv[:, :LANES]
        accl_ref[rows, :] = al * accl_ref[rows, :] + pv[:, LANES:]

    @pl.when(qi == 0)
    def _():
        scores(0, buf_a, "diag")

    @pl.when(qi > 0)
    def _():
        scores(0, buf_a, "full")

    def pair(t, next_kind):
        scores(2 * t + 1, buf_b, "full")
        accumulate(2 * t, buf_a)
        scores(2 * t + 2, buf_a, next_kind)
        accumulate(2 * t + 1, buf_b)

    n_loop = jnp.maximum(qi - 1, 0)
    odd = n_loop % 2

    @pl.when(odd == 1)
    def _():
        pair(0, "full")

    def body(u, c):
        t = odd + 2 * u
        pair(t, "full")
        pair(t + 1, "full")
        return c

    lax.fori_loop(0, n_loop // 2, body, 0)

    @pl.when(qi > 0)
    def _():
        pair(qi - 1, "diag")

    scores(2 * qi + 1, buf_b, "last")
    accumulate(2 * qi, buf_a)
    accumulate(2 * qi + 1, buf_b, late)

    o = acco_ref[...] / accl_ref[...]
    lam = lam_ref[0]
    o = jnp.concatenate([o[:tk] - lam * o[tk:2 * tk], o[2 * tk:3 * tk] - lam * o[3 * tk:]], axis=0)
    o_ref[...] = (_rms(o, g_ref[...]) * (1.0 - lam_init)).astype(o_ref.dtype)


def _attn_layer(lam, qq, kk, v, g, lam_init):
    n = qq.shape[1]
    tq = min(1024, n)
    tk = tq // 2
    assert n % tq == 0 and tk % CHUNK == 0 and tk % LANES == 0
    stat = pltpu.VMEM((2 * tq, LANES), F32)
    return pl.pallas_call(
        functools.partial(_attn_kernel, tq=tq, tk=tk, lam_init=lam_init),
        grid=(N_HEADS, n // tq),
        in_specs=[pl.BlockSpec(memory_space=pltpu.SMEM),
                  pl.BlockSpec((1, tq, LANES), lambda h, i: (h, i, 0)),
                  pl.BlockSpec((1, n, LANES), lambda h, i: (h, 0, 0)),
                  pl.BlockSpec((1, n, LANES), lambda h, i: (h, 0, 0)),
                  pl.BlockSpec((1, LANES), lambda h, i: (0, 0))],
        out_specs=pl.BlockSpec((tq, LANES), lambda h, i: (i, h)),
        out_shape=jax.ShapeDtypeStruct((n, N_HEADS * V_DIM), BF16),
        scratch_shapes=[pltpu.VMEM((2 * tq, LANES), BF16),
                        pltpu.VMEM((2 * tq, tk), F32), pltpu.VMEM((2 * tq, tk), F32),
                        stat, stat, stat, stat, stat, stat, stat],
        compiler_params=_cparams(("parallel", "arbitrary")),
        name="diff_attn",
    )(lam, qq, kk, v, g)


def kernel(x, norm_mix_g, norm_ffn_g, ffn_w1, ffn_w3, ffn_w2, ssm_lam_re, ssm_lam_im, ssm_log_dt, ssm_b_re, ssm_b_im, ssm_c_re, ssm_c_im, ssm_d, ssm_w_glu, kv_norm_g, w_kv, attn_w_q, attn_lq1, attn_lk1, attn_lq2, attn_lk2, attn_subln_g, attn_w_o, final_norm_g):
    bsz, L, _ = x.shape
    assert bsz == 1 and L % (S5_CHUNK * 8) == 0
    row = lambda t: t.reshape(1, -1).astype(F32)

    cw1, cwout, a_chunk = _s5_weights(ssm_lam_re[0], ssm_lam_im[0], ssm_log_dt[0], ssm_b_re[0], ssm_b_im[0],
                                      ssm_c_re[0], ssm_c_im[0])
    stp, ptab = _scan_powers(a_chunk)
    h = _s5_layer(x.reshape(L, D_MODEL), row(norm_mix_g[0]), cw1, cwout, stp, ptab, row(ssm_d[0]),
                  ssm_w_glu[0].astype(BF16))
    h = _ffn_layer(h, row(norm_ffn_g[0]), ffn_w1[0].astype(BF16), ffn_w3[0].astype(BF16), ffn_w2[0].astype(BF16))

    hk = N_HEADS * HEAD_DIM
    cs, sel, one = _rope_tables(L)
    kk, v, qq = _proj_layer(h, row(kv_norm_g), row(norm_mix_g[1]), w_kv.astype(BF16), attn_w_q[0].astype(BF16),
                            cs, sel, one)
    lam_init = 0.8 - 0.6 * math.exp(-0.3 * N_A_LAYERS)
    lam = (jnp.exp(jnp.sum(attn_lq1[0].astype(F32) * attn_lk1[0].astype(F32)))
           - jnp.exp(jnp.sum(attn_lq2[0].astype(F32) * attn_lk2[0].astype(F32))) + lam_init).reshape(1)
    a = _attn_layer(lam, qq, kk, v, row(attn_subln_g[0]), lam_init)
    out = _attn_out_ffn_layer(h, a, attn_w_o[0].astype(BF16), row(norm_ffn_g[1]), ffn_w1[1].astype(BF16),
                              ffn_w3[1].astype(BF16), ffn_w2[1].astype(BF16), row(final_norm_g))
    return out.reshape(1, L, D_MODEL)
```

```python
import functools
import math

import jax
import jax.numpy as jnp
import numpy as np
from jax import lax
from jax.experimental import pallas as pl
from jax.experimental.pallas import tpu as pltpu

F32 = jnp.float32
BF16 = jnp.bfloat16

D_MODEL = 1024
CHUNK = 64
SSM_GROUP = 16
N_GROUPS = D_MODEL // SSM_GROUP
SSM_STATE = 64
N_HEADS = 8
HEAD_DIM = 64
V_DIM = 2 * HEAD_DIM
ROT_DIM = HEAD_DIM // 4
ROPE_THETA = 500000.0
FFN_HIDDEN = 2816
EPS = 1e-6
N_A_LAYERS = 1

LANES = 128
SUBLANES = 8
S5_CHUNK = 8
N_SLABS = D_MODEL // LANES
GROUPS_PER_SLAB = LANES // SSM_GROUP
SLAB_STATE = GROUPS_PER_SLAB * SSM_STATE
SLAB_W = S5_CHUNK * LANES
SCAN_STEPS = 3
FFN_TILE = 256
Q_SCALE = HEAD_DIM ** -0.5 * math.log2(math.e)
VMEM_LIMIT = 56 * 1024 * 1024


def _cparams(sem):
    return pltpu.CompilerParams(dimension_semantics=sem, vmem_limit_bytes=VMEM_LIMIT)


def _rms(x, g):
    return x * lax.rsqrt(jnp.mean(x * x, axis=-1, keepdims=True) + EPS) * g


def _s5_weights(lam_re, lam_im, log_dt, b_re, b_im, c_re, c_im):
    hi = lax.Precision.HIGHEST
    lr, li = lam_re.astype(F32), lam_im.astype(F32)
    dt = jnp.exp(log_dt.astype(F32))[:, None]
    mag = jnp.exp(lr * dt)
    ar, ai = mag * jnp.cos(li * dt), mag * jnp.sin(li * dt)
    nr, ni = ar - 1.0, ai
    den = lr * lr + li * li
    fr = (nr * lr + ni * li) / den
    fi = (ni * lr - nr * li) / den
    br, bi = b_re.astype(F32), b_im.astype(F32)
    bbr = fr[..., None] * br - fi[..., None] * bi
    bbi = fr[..., None] * bi + fi[..., None] * br
    cr, ci = c_re.astype(F32), c_im.astype(F32)

    prs, pis = [jnp.ones_like(ar)], [jnp.zeros_like(ar)]
    for _ in range(S5_CHUNK):
        prs.append(prs[-1] * ar - pis[-1] * ai)
        pis.append(prs[-2] * ai + pis[-1] * ar)
    pr = jnp.stack(prs)
    pi = jnp.stack(pis)

    T = S5_CHUNK
    kr = pr[:T, :, :, None] * bbr[None] - pi[:T, :, :, None] * bbi[None]
    ki = pr[:T, :, :, None] * bbi[None] + pi[:T, :, :, None] * bbr[None]
    klag = (jnp.einsum('gdp,kgpc->kgcd', cr, kr, precision=hi)
            - jnp.einsum('gdp,kgpc->kgcd', ci, ki, precision=hi))
    toe = jnp.stack([jnp.concatenate([jnp.zeros_like(klag[:j]), klag[:T - j]], axis=0) for j in range(T)])
    win_r = kr[::-1]
    win_i = ki[::-1]
    p1r, p1i = pr[1:], pi[1:]
    out_r = cr[None] * p1r[:, :, None, :] - ci[None] * p1i[:, :, None, :]
    out_i = -cr[None] * p1i[:, :, None, :] - ci[None] * p1r[:, :, None, :]

    S, Gs = N_SLABS, GROUPS_PER_SLAB
    c_toe = toe.reshape(T, T, S, Gs, SSM_GROUP, SSM_GROUP).transpose(2, 0, 3, 4, 1, 5).reshape(S, SLAB_W, LANES)
    win = jnp.concatenate([win_r, win_i], axis=2)
    c_win = win.reshape(T, S, Gs, 2 * SSM_STATE, SSM_GROUP).transpose(1, 0, 2, 4, 3).reshape(S, SLAB_W, LANES)
    cw1 = jnp.concatenate([c_toe, c_win], axis=-1).astype(BF16)
    to_out = lambda w: w.reshape(T, S, Gs, SSM_GROUP, SSM_STATE).transpose(1, 2, 4, 0, 3).reshape(S, SLAB_STATE, LANES)
    cwout = jnp.concatenate([to_out(out_r), to_out(out_i)], axis=1).astype(BF16)
    a_chunk = (pr[T].reshape(S, SLAB_STATE), pi[T].reshape(S, SLAB_STATE))
    return cw1, cwout, a_chunk


def _expansion_matrix():
    e = np.zeros((2 * LANES, SLAB_W + 2 * SLAB_STATE), np.float32)
    for h in range(GROUPS_PER_SLAB):
        for t in range(S5_CHUNK):
            for d in range(SSM_GROUP):
                e[t * SSM_GROUP + d, t * LANES + h * SSM_GROUP + d] = 1.0
        for p in range(SSM_STATE):
            e[LANES + p, SLAB_W + h * SSM_STATE + p] = 1.0
            e[LANES + SSM_STATE + p, SLAB_W + SLAB_STATE + h * SSM_STATE + p] = 1.0
    return jnp.asarray(e, BF16)


def _scan_powers(a_chunk):
    ar, ai = a_chunk
    r, i = ar, ai
    steps, tile_r, tile_i = [], [], []
    in_tile = jnp.arange(SUBLANES)[None, :, None]
    for s in range(SCAN_STEPS):
        on = in_tile >= (1 << s)
        steps.append(jnp.stack([jnp.where(on, r[:, None, :], 0.0), jnp.where(on, i[:, None, :], 0.0)], axis=1))
        r, i = r * r - i * i, 2.0 * r * i
    r, i = ar, ai
    for _ in range(SUBLANES):
        tile_r.append(r)
        tile_i.append(i)
        r, i = r * ar - i * ai, r * ai + i * ar
    return jnp.stack(steps, axis=1), jnp.stack([jnp.stack(tile_r, axis=1), jnp.stack(tile_i, axis=1)], axis=1)


def _s5_pre_kernel(x_ref, g_ref, u_ref):
    nb = u_ref.shape[1]
    u = _rms(x_ref[...], g_ref[...])
    uj = jnp.swapaxes(u.reshape(nb, S5_CHUNK, D_MODEL), 0, 1)
    for j in range(S5_CHUNK):
        for s in range(N_SLABS):
            u_ref[s, :, j * LANES:(j + 1) * LANES] = uj[j, :, s * LANES:(s + 1) * LANES].astype(BF16)


def _expand_block_diag(c, e_ref, w_ref, row_period, row_group):
    n_rows = c.shape[0]
    tile = SLAB_STATE
    for ct in range(w_ref.shape[1] // tile):
        x = jnp.dot(c, e_ref[:c.shape[1], ct * tile:(ct + 1) * tile], preferred_element_type=F32)
        row = lax.broadcasted_iota(jnp.int32, (n_rows, tile), 0)
        col = lax.broadcasted_iota(jnp.int32, (n_rows, tile), 1)
        if ct * tile < SLAB_W:
            col_g = (col % LANES) // SSM_GROUP
        else:
            col_g = col // SSM_STATE
        keep = (row % row_period) // row_group == col_g
        w_ref[:, ct * tile:(ct + 1) * tile] = jnp.where(keep, x, 0.0).astype(w_ref.dtype)


def _s5_core_kernel(u_ref, cw1_ref, cwout_ref, e_ref, stp_ref, ptab_ref, y_ref, w1_ref, wout_ref, ra_ref, rb_ref,
                    *, nb):
    _expand_block_diag(cw1_ref[0], e_ref, w1_ref, LANES, SSM_GROUP)
    _expand_block_diag(cwout_ref[0], e_ref, wout_ref, SLAB_STATE, SSM_STATE)
    n_blocks = u_ref.shape[1] // nb
    n_tiles = nb // SUBLANES
    tile = (SUBLANES, SLAB_STATE)
    in_tile = lax.broadcasted_iota(jnp.int32, tile, 0)
    pt_r, pt_i = ptab_ref[0, 0], ptab_ref[0, 1]

    def inject(b, r_ref):
        r_ref[...] = jnp.dot(u_ref[0, b * nb:(b + 1) * nb, :], w1_ref[...], preferred_element_type=F32)

    inject(0, ra_ref)
    c_r = jnp.zeros((1, SLAB_STATE), F32)
    c_i = jnp.zeros((1, SLAB_STATE), F32)
    for b in range(n_blocks):
        cur, nxt = (ra_ref, rb_ref) if b % 2 == 0 else (rb_ref, ra_ref)
        if b + 1 < n_blocks:
            inject(b + 1, nxt)
        er = cur[:, SLAB_W:SLAB_W + SLAB_STATE].reshape((n_tiles,) + tile)
        ei = cur[:, SLAB_W + SLAB_STATE:].reshape((n_tiles,) + tile)
        for i in range(SCAN_STEPS):
            p_r, p_i = stp_ref[0, i, 0], stp_ref[0, i, 1]
            sr, si = pltpu.roll(er, 1 << i, 1), pltpu.roll(ei, 1 << i, 1)
            er, ei = er + p_r * sr - p_i * si, ei + p_r * si + p_i * sr
        pr, pi = [], []
        for k in range(n_tiles):
            b_r = jnp.broadcast_to(c_r, tile)
            b_i = jnp.broadcast_to(c_i, tile)
            t_r = er[k] + pt_r * b_r - pt_i * b_i
            t_i = ei[k] + pt_r * b_i + pt_i * b_r
            pr.append(jnp.where(in_tile == 0, b_r, pltpu.roll(t_r, 1, 0)))
            pi.append(jnp.where(in_tile == 0, b_i, pltpu.roll(t_i, 1, 0)))
            c_r, c_i = t_r[SUBLANES - 1:], t_i[SUBLANES - 1:]
        prev = jnp.concatenate([jnp.concatenate(pr, axis=0), jnp.concatenate(pi, axis=0)], axis=1).astype(BF16)
        y_ref[0, b * nb:(b + 1) * nb, :] = cur[:, :SLAB_W] + jnp.dot(prev, wout_ref[...],
                                                                      preferred_element_type=F32)


def _s5_post_kernel(x_ref, y_ref, g_ref, d_ref, wglu_ref, o_ref):
    nb = y_ref.shape[1] // 2
    for half in range(2):
        rs = slice(half * nb, (half + 1) * nb)
        ps = slice(half * S5_CHUNK * nb, (half + 1) * S5_CHUNK * nb)
        yj = jnp.stack([jnp.concatenate([y_ref[s, rs, j * LANES:(j + 1) * LANES] for s in range(N_SLABS)], axis=1)
                        for j in range(S5_CHUNK)], axis=0)
        y = jnp.swapaxes(yj, 0, 1).reshape(S5_CHUNK * nb, D_MODEL)
        x = x_ref[ps, :]
        z = jax.nn.gelu(y + d_ref[...] * _rms(x, g_ref[...]), approximate=True).astype(BF16)
        zz = jnp.dot(z, wglu_ref[...], preferred_element_type=F32)
        za, zb = zz[:, :D_MODEL], zz[:, D_MODEL:]
        o_ref[ps, :] = x + za * (1.0 / (1.0 + jnp.exp(-zb)))


def _s5_layer(x, g_mix, cw1, cwout, stp, ptab, d_skip, w_glu):
    n_rows = x.shape[0] // S5_CHUNK
    nb = min(128, n_rows)
    nbp = min(128, n_rows)
    nbc = min(256, n_rows)
    assert n_rows % nb == 0 and n_rows % nbc == 0 and nbp % 16 == 0

    u = pl.pallas_call(
        _s5_pre_kernel,
        grid=(n_rows // nb,),
        in_specs=[pl.BlockSpec((S5_CHUNK * nb, D_MODEL), lambda i: (i, 0)),
                  pl.BlockSpec((1, D_MODEL), lambda i: (0, 0))],
        out_specs=pl.BlockSpec((N_SLABS, nb, SLAB_W), lambda i: (0, i, 0)),
        out_shape=jax.ShapeDtypeStruct((N_SLABS, n_rows, SLAB_W), BF16),
        compiler_params=_cparams(("parallel",)),
        name="s5_pre",
    )(x, g_mix)

    y = pl.pallas_call(
        functools.partial(_s5_core_kernel, nb=nbc),
        grid=(N_SLABS,),
        in_specs=[pl.BlockSpec((1, n_rows, SLAB_W), lambda s: (s, 0, 0)),
                  pl.BlockSpec((1, SLAB_W, 2 * LANES), lambda s: (s, 0, 0)),
                  pl.BlockSpec((1, 2 * SLAB_STATE, LANES), lambda s: (s, 0, 0)),
                  pl.BlockSpec((2 * LANES, SLAB_W + 2 * SLAB_STATE), lambda s: (0, 0)),
                  pl.BlockSpec((1,) + stp.shape[1:], lambda s: (s, 0, 0, 0, 0)),
                  pl.BlockSpec((1,) + ptab.shape[1:], lambda s: (s, 0, 0, 0))],
        out_specs=pl.BlockSpec((1, n_rows, SLAB_W), lambda s: (s, 0, 0)),
        out_shape=jax.ShapeDtypeStruct((N_SLABS, n_rows, SLAB_W), F32),
        scratch_shapes=[pltpu.VMEM((SLAB_W, SLAB_W + 2 * SLAB_STATE), BF16),
                        pltpu.VMEM((2 * SLAB_STATE, SLAB_W), BF16),
                        pltpu.VMEM((nbc, SLAB_W + 2 * SLAB_STATE), F32),
                        pltpu.VMEM((nbc, SLAB_W + 2 * SLAB_STATE), F32)],
        compiler_params=_cparams(("parallel",)),
        name="s5_core",
    )(u, cw1, cwout, _expansion_matrix(), stp, ptab)

    return pl.pallas_call(
        _s5_post_kernel,
        grid=(n_rows // nbp,),
        in_specs=[pl.BlockSpec((S5_CHUNK * nbp, D_MODEL), lambda i: (i, 0)),
                  pl.BlockSpec((N_SLABS, nbp, SLAB_W), lambda i: (0, i, 0)),
                  _const_spec((1, D_MODEL)),
                  _const_spec((1, D_MODEL)),
                  _const_spec((D_MODEL, 2 * D_MODEL))],
        out_specs=pl.BlockSpec((S5_CHUNK * nbp, D_MODEL), lambda i: (i, 0)),
        out_shape=jax.ShapeDtypeStruct(x.shape, F32),
        compiler_params=_cparams(("parallel",)),
        name="s5_post",
    )(x, y, g_mix, d_skip, w_glu)


def _ffn_body(h, g, w1_ref, w3_ref, w2_ref):
    hn = _rms(h, g).astype(BF16)
    acc = jnp.zeros(h.shape, F32)
    for f in range(FFN_HIDDEN // FFN_TILE):
        sl = slice(f * FFN_TILE, (f + 1) * FFN_TILE)
        a = jnp.dot(hn, w1_ref[:, sl], preferred_element_type=F32)
        b = jnp.dot(hn, w3_ref[:, sl], preferred_element_type=F32)
        t = (a * (1.0 / (1.0 + jnp.exp(-a))) * b).astype(BF16)
        acc = acc + jnp.dot(t, w2_ref[sl, :], preferred_element_type=F32)
    return h + acc


def _ffn_kernel(h_ref, g_ref, w1_ref, w3_ref, w2_ref, o_ref):
    o_ref[...] = _ffn_body(h_ref[...], g_ref[...], w1_ref, w3_ref, w2_ref)


def _attn_out_ffn_kernel(h_ref, a_ref, wo_ref, g_ref, w1_ref, w3_ref, w2_ref, gf_ref, o_ref):
    h = h_ref[...] + jnp.dot(a_ref[...], wo_ref[...], preferred_element_type=F32)
    o_ref[...] = _rms(_ffn_body(h, g_ref[...], w1_ref, w3_ref, w2_ref), gf_ref[...])


def _const_spec(shape):
    return pl.BlockSpec(shape, lambda i: (0,) * len(shape), pipeline_mode=pl.Buffered(1))


def _layer_spec(w, layer):
    return pl.BlockSpec((None,) + w.shape[1:], lambda i: (layer, 0, 0), pipeline_mode=pl.Buffered(1))


def _ffn_layer(h, g, w1, w3, w2, layer):
    n, tm = h.shape[0], min(512, h.shape[0])
    return pl.pallas_call(
        _ffn_kernel,
        grid=(n // tm,),
        in_specs=[pl.BlockSpec((tm, D_MODEL), lambda i: (i, 0)),
                  _const_spec((1, D_MODEL)),
                  _layer_spec(w1, layer), _layer_spec(w3, layer), _layer_spec(w2, layer)],
        out_specs=pl.BlockSpec((tm, D_MODEL), lambda i: (i, 0)),
        out_shape=jax.ShapeDtypeStruct(h.shape, F32),
        compiler_params=_cparams(("parallel",)),
        name="ffn",
    )(h, g, w1, w3, w2)


def _attn_out_ffn_layer(h, a, wo, g, w1, w3, w2, gf, layer):
    n, tm = h.shape[0], min(512, h.shape[0])
    return pl.pallas_call(
        _attn_out_ffn_kernel,
        grid=(n // tm,),
        in_specs=[pl.BlockSpec((tm, D_MODEL), lambda i: (i, 0)),
                  pl.BlockSpec((tm, D_MODEL), lambda i: (i, 0)),
                  _const_spec((D_MODEL, D_MODEL)),
                  _const_spec((1, D_MODEL)),
                  _layer_spec(w1, layer), _layer_spec(w3, layer), _layer_spec(w2, layer),
                  _const_spec((1, D_MODEL))],
        out_specs=pl.BlockSpec((tm, D_MODEL), lambda i: (i, 0)),
        out_shape=jax.ShapeDtypeStruct(h.shape, F32),
        compiler_params=_cparams(("parallel",)),
        name="attn_out_ffn",
    )(h, a, wo, g, w1, w3, w2, gf)


def _rope(t, cos, sin_lo, sin_hi):
    return t * cos + pltpu.roll(t, LANES - ROT_DIM // 2, 1) * sin_lo + pltpu.roll(t, ROT_DIM // 2, 1) * sin_hi


def _pair_heads(x, hd, lane):
    hk = N_HEADS * HEAD_DIM
    c0 = (hd // 2) * LANES
    t1, t2 = x[:, c0:c0 + LANES], x[:, hk + c0:hk + c0 + LANES]
    if hd % 2 == 0:
        return jnp.where(lane < HEAD_DIM, t1, pltpu.roll(t2, HEAD_DIM, 1))
    return jnp.where(lane < HEAD_DIM, pltpu.roll(t1, HEAD_DIM, 1), t2)


def _proj_kernel(h_ref, gkv_ref, gq_ref, wkv_ref, wq_ref, cs_ref, sel_ref, one_ref, kk_ref, v_ref, qq_ref):
    tm = h_ref.shape[0] // 2
    lane = lax.broadcasted_iota(jnp.int32, (tm, LANES), 1)
    for half in range(2):
        rs = slice(half * tm, (half + 1) * tm)
        h = h_ref[rs, :]
        tab = jnp.dot(cs_ref[rs, :], sel_ref[...], preferred_element_type=F32)
        cos, slo, shi = tab[:, :LANES] + one_ref[...], tab[:, LANES:2 * LANES], tab[:, 2 * LANES:]
        kv = jnp.dot(_rms(h, gkv_ref[...]).astype(BF16), wkv_ref[...], preferred_element_type=F32)
        q = jnp.dot(_rms(h, gq_ref[...]).astype(BF16), wq_ref[...], preferred_element_type=F32)
        for hd in range(N_HEADS):
            kk_ref[hd, rs, :] = _rope(_pair_heads(kv, hd, lane), cos, slo, shi).astype(BF16)
            qq_ref[hd, rs, :] = (_rope(_pair_heads(q, hd, lane), cos, slo, shi) * Q_SCALE).astype(BF16)
            v_ref[hd, rs, :] = kv[:, D_MODEL + hd * LANES:D_MODEL + (hd + 1) * LANES].astype(BF16)


def _proj_layer(h, gkv, gq, wkv, wq, cs, sel, one):
    n, tm = h.shape[0], min(1024, h.shape[0])
    hshape = jax.ShapeDtypeStruct((N_HEADS, n, LANES), BF16)
    hspec = pl.BlockSpec((N_HEADS, tm, LANES), lambda i: (0, i, 0))
    return pl.pallas_call(
        _proj_kernel,
        grid=(n // tm,),
        in_specs=[pl.BlockSpec((tm, D_MODEL), lambda i: (i, 0)),
                  _const_spec((1, D_MODEL)), _const_spec((1, D_MODEL)),
                  _const_spec((D_MODEL, 2 * D_MODEL)), _const_spec((D_MODEL, D_MODEL)),
                  pl.BlockSpec((tm, cs.shape[1]), lambda i: (i, 0)),
                  _const_spec(sel.shape), _const_spec((1, LANES))],
        out_specs=[hspec, hspec, hspec],
        out_shape=[hshape, hshape, hshape],
        compiler_params=_cparams(("parallel",)),
        name="qkv_proj",
    )(h, gkv, gq, wkv, wq, cs, sel, one)


def _rope_tables(n):
    half = ROT_DIM // 2
    inv_freq = ROPE_THETA ** (-jnp.arange(half, dtype=F32) * 2.0 / ROT_DIM)
    ang = inv_freq[:, None] * jnp.arange(n, dtype=jnp.int32).astype(F32)[None, :]
    cs = jnp.concatenate([jnp.cos(ang), jnp.sin(ang)], axis=0).T
    hi = cs.astype(BF16)
    lo = (cs - hi.astype(F32)).astype(BF16)
    sel = np.zeros((ROT_DIM, 3 * LANES), np.float32)
    one = np.ones((1, LANES), np.float32)
    for l in range(LANES):
        d = l % HEAD_DIM
        if d < ROT_DIM:
            one[0, l] = 0.0
            sel[d % half, l] = 1.0
            if d < half:
                sel[half + d, LANES + l] = -1.0
            else:
                sel[half + d - half, 2 * LANES + l] = 1.0
    sel = np.concatenate([sel, sel], axis=0)
    return jnp.concatenate([hi, lo], axis=1), jnp.asarray(sel, BF16), jnp.asarray(one)


def _attn_kernel(lam_ref, qq_ref, kk_ref, v_ref, g_ref, o_ref, q2_ref, sa_ref, sb_ref, ma_ref, mb_ref,
                 ala_ref, alb_ref, mrun_ref, acco_ref, accl_ref, *, tq, tk, lam_init):
    qi = pl.program_id(1)
    lane = lax.broadcasted_iota(jnp.int32, (tk, LANES), 1)
    zero = jnp.zeros((tk, LANES), BF16)
    for half in range(2):
        q = qq_ref[0, half * tk:(half + 1) * tk, :]
        q2_ref[(2 * half) * tk:(2 * half + 1) * tk, :] = jnp.where(lane < HEAD_DIM, q, zero)
        q2_ref[(2 * half + 1) * tk:(2 * half + 2) * tk, :] = jnp.where(lane >= HEAD_DIM, q, zero)
    mrun_ref[...] = jnp.full(mrun_ref.shape, -jnp.inf, F32)
    acco_ref[...] = jnp.zeros(acco_ref.shape, F32)
    accl_ref[...] = jnp.zeros(accl_ref.shape, F32)
    buf_a = (sa_ref, ma_ref, ala_ref)
    buf_b = (sb_ref, mb_ref, alb_ref)
    late = slice(2 * tk, 4 * tk)

    def causal(s):
        r = lax.broadcasted_iota(jnp.int32, s.shape, 0)
        c = lax.broadcasted_iota(jnp.int32, s.shape, 1)
        return jnp.where((c // CHUNK) <= ((r % tk) // CHUNK), s, -jnp.inf)

    def stats(s, rows, buf):
        s_ref, m_ref, al_ref = buf
        s_ref[rows, :] = s
        m_prev = mrun_ref[rows, :]
        m_new = jnp.maximum(m_prev, jnp.max(s, axis=1, keepdims=True))
        al_ref[rows, :] = jnp.exp2(m_prev - m_new)
        m_ref[rows, :] = m_new
        mrun_ref[rows, :] = m_new

    def scores(blk, buf, kind):
        k = kk_ref[0, pl.ds(pl.multiple_of(blk * tk, tk), tk), :]
        dims = (((1,), (1,)), ((), ()))
        if kind == "last":
            stats(causal(lax.dot_general(q2_ref[late, :], k, dims, preferred_element_type=F32)), late, buf)
            return
        s = lax.dot_general(q2_ref[...], k, dims, preferred_element_type=F32)
        if kind == "diag":
            stats(causal(s[:2 * tk]), slice(0, 2 * tk), buf)
            stats(s[2 * tk:], late, buf)
        else:
            stats(s, slice(None), buf)

    def accumulate(blk, buf, rows=slice(None)):
        s_ref, m_ref, al_ref = buf
        v = v_ref[0, pl.ds(pl.multiple_of(blk * tk, tk), tk), :]
        v1 = jnp.concatenate([v, jnp.ones_like(v)], axis=1)
        m = m_ref[rows, :]
        p = jnp.concatenate([jnp.exp2(s_ref[rows, j * LANES:(j + 1) * LANES] - m) for j in range(tk // LANES)],
                            axis=1).astype(BF16)
        pv = jnp.dot(p, v1, preferred_element_type=F32)
        al = al_ref[rows, :]
        acco_ref[rows, :] = al * acco_ref[rows, :] + pv[:, :LANES]
        accl_ref[rows, :] = al * accl_ref[rows, :] + pv[:, LANES:]

    @pl.when(qi == 0)
    def _():
        scores(0, buf_a, "diag")

    @pl.when(qi > 0)
    def _():
        scores(0, buf_a, "full")

    def pair(t, next_kind):
        scores(2 * t + 1, buf_b, "full")
        accumulate(2 * t, buf_a)
        scores(2 * t + 2, buf_a, next_kind)
        accumulate(2 * t + 1, buf_b)

    n_loop = jnp.maximum(qi - 1, 0)
    odd = n_loop % 2

    @pl.when(odd == 1)
    def _():
        pair(0, "full")

    def body(u, c):
        t = odd + 2 * u
        pair(t, "full")
        pair(t + 1, "full")
        return c

    lax.fori_loop(0, n_loop // 2, body, 0)

    @pl.when(qi > 0)
    def _():
        pair(qi - 1, "diag")

    scores(2 * qi + 1, buf_b, "last")
    accumulate(2 * qi, buf_a)
    accumulate(2 * qi + 1, buf_b, late)

    o = acco_ref[...] / accl_ref[...]
    lam = lam_ref[0]
    o = jnp.concatenate([o[:tk] - lam * o[tk:2 * tk], o[2 * tk:3 * tk] - lam * o[3 * tk:]], axis=0)
    o_ref[...] = (_rms(o, g_ref[...]) * (1.0 - lam_init)).astype(o_ref.dtype)


def _attn_layer(lam, qq, kk, v, g, lam_init):
    n = qq.shape[1]
    tq = min(1024, n)
    tk = tq // 2
    assert n % tq == 0 and tk % CHUNK == 0 and tk % LANES == 0
    stat = pltpu.VMEM((2 * tq, LANES), F32)
    return pl.pallas_call(
        functools.partial(_attn_kernel, tq=tq, tk=tk, lam_init=lam_init),
        grid=(N_HEADS, n // tq),
        in_specs=[pl.BlockSpec(memory_space=pltpu.SMEM),
                  pl.BlockSpec((1, tq, LANES), lambda h, i: (h, i, 0)),
                  pl.BlockSpec((1, n, LANES), lambda h, i: (h, 0, 0)),
                  pl.BlockSpec((1, n, LANES), lambda h, i: (h, 0, 0)),
                  pl.BlockSpec((1, LANES), lambda h, i: (0, 0))],
        out_specs=pl.BlockSpec((tq, LANES), lambda h, i: (i, h)),
        out_shape=jax.ShapeDtypeStruct((n, N_HEADS * V_DIM), BF16),
        scratch_shapes=[pltpu.VMEM((2 * tq, LANES), BF16),
                        pltpu.VMEM((2 * tq, tk), F32), pltpu.VMEM((2 * tq, tk), F32),
                        stat, stat, stat, stat, stat, stat, stat],
        compiler_params=_cparams(("parallel", "arbitrary")),
        name="diff_attn",
    )(lam, qq, kk, v, g)


def kernel(x, norm_mix_g, norm_ffn_g, ffn_w1, ffn_w3, ffn_w2, ssm_lam_re, ssm_lam_im, ssm_log_dt, ssm_b_re, ssm_b_im, ssm_c_re, ssm_c_im, ssm_d, ssm_w_glu, kv_norm_g, w_kv, attn_w_q, attn_lq1, attn_lk1, attn_lq2, attn_lk2, attn_subln_g, attn_w_o, final_norm_g):
    bsz, L, _ = x.shape
    assert bsz == 1 and L % (S5_CHUNK * 8) == 0
    row = lambda t: t.reshape(1, -1).astype(F32)

    cw1, cwout, a_chunk = _s5_weights(ssm_lam_re[0], ssm_lam_im[0], ssm_log_dt[0], ssm_b_re[0], ssm_b_im[0],
                                      ssm_c_re[0], ssm_c_im[0])
    stp, ptab = _scan_powers(a_chunk)
    h = _s5_layer(x.reshape(L, D_MODEL), row(norm_mix_g[0]), cw1, cwout, stp, ptab, row(ssm_d[0]),
                  ssm_w_glu[0].astype(BF16))
    fw1, fw3, fw2 = ffn_w1.astype(BF16), ffn_w3.astype(BF16), ffn_w2.astype(BF16)
    h = _ffn_layer(h, row(norm_ffn_g[0]), fw1, fw3, fw2, 0)

    cs, sel, one = _rope_tables(L)
    kk, v, qq = _proj_layer(h, row(kv_norm_g), row(norm_mix_g[1]), w_kv.astype(BF16), attn_w_q[0].astype(BF16),
                            cs, sel, one)
    lam_init = 0.8 - 0.6 * math.exp(-0.3 * N_A_LAYERS)
    lam = (jnp.exp(jnp.sum(attn_lq1[0].astype(F32) * attn_lk1[0].astype(F32)))
           - jnp.exp(jnp.sum(attn_lq2[0].astype(F32) * attn_lk2[0].astype(F32))) + lam_init).reshape(1)
    a = _attn_layer(lam, qq, kk, v, row(attn_subln_g[0]), lam_init)
    out = _attn_out_ffn_layer(h, a, attn_w_o[0].astype(BF16), row(norm_ffn_g[1]), fw1, fw3, fw2,
                              row(final_norm_g), 1)
    return out.reshape(1, L, D_MODEL)
```

```python
import functools
import math

import jax
import jax.numpy as jnp
import numpy as np
from jax import lax
from jax.experimental import pallas as pl
from jax.experimental.pallas import tpu as pltpu

F32 = jnp.float32
BF16 = jnp.bfloat16

D_MODEL = 1024
CHUNK = 64
SSM_GROUP = 16
N_GROUPS = D_MODEL // SSM_GROUP
SSM_STATE = 64
N_HEADS = 8
HEAD_DIM = 64
V_DIM = 2 * HEAD_DIM
ROT_DIM = HEAD_DIM // 4
ROPE_THETA = 500000.0
FFN_HIDDEN = 2816
EPS = 1e-6
N_A_LAYERS = 1

LANES = 128
SUBLANES = 8
S5_CHUNK = 8
N_SLABS = D_MODEL // LANES
GROUPS_PER_SLAB = LANES // SSM_GROUP
SLAB_STATE = GROUPS_PER_SLAB * SSM_STATE
SLAB_W = S5_CHUNK * LANES
SCAN_STEPS = 3
FFN_TILE = 256
Q_SCALE = HEAD_DIM ** -0.5 * math.log2(math.e)
VMEM_LIMIT = 56 * 1024 * 1024


def _cparams(sem):
    return pltpu.CompilerParams(dimension_semantics=sem, vmem_limit_bytes=VMEM_LIMIT)


def _rms(x, g):
    return x * lax.rsqrt(jnp.mean(x * x, axis=-1, keepdims=True) + EPS) * g


def _s5_weights(lam_re, lam_im, log_dt, b_re, b_im, c_re, c_im):
    hi = lax.Precision.HIGHEST
    lr, li = lam_re.astype(F32), lam_im.astype(F32)
    dt = jnp.exp(log_dt.astype(F32))[:, None]
    mag = jnp.exp(lr * dt)
    ar, ai = mag * jnp.cos(li * dt), mag * jnp.sin(li * dt)
    nr, ni = ar - 1.0, ai
    den = lr * lr + li * li
    fr = (nr * lr + ni * li) / den
    fi = (ni * lr - nr * li) / den
    br, bi = b_re.astype(F32), b_im.astype(F32)
    bbr = fr[..., None] * br - fi[..., None] * bi
    bbi = fr[..., None] * bi + fi[..., None] * br
    cr, ci = c_re.astype(F32), c_im.astype(F32)

    prs, pis = [jnp.ones_like(ar)], [jnp.zeros_like(ar)]
    for _ in range(S5_CHUNK):
        prs.append(prs[-1] * ar - pis[-1] * ai)
        pis.append(prs[-2] * ai + pis[-1] * ar)
    pr = jnp.stack(prs)
    pi = jnp.stack(pis)

    T = S5_CHUNK
    kr = pr[:T, :, :, None] * bbr[None] - pi[:T, :, :, None] * bbi[None]
    ki = pr[:T, :, :, None] * bbi[None] + pi[:T, :, :, None] * bbr[None]
    klag = (jnp.einsum('gdp,kgpc->gckd', cr, kr, precision=hi)
            - jnp.einsum('gdp,kgpc->gckd', ci, ki, precision=hi)).reshape(N_GROUPS, SSM_GROUP, LANES)
    toe = jnp.stack([jnp.pad(klag, ((0, 0), (0, 0), (SSM_GROUP * j, 0)))[..., :LANES] for j in range(T)])
    win_r = kr[::-1]
    win_i = ki[::-1]
    p1r, p1i = pr[1:], pi[1:]
    out_r = cr[None] * p1r[:, :, None, :] - ci[None] * p1i[:, :, None, :]
    out_i = -cr[None] * p1i[:, :, None, :] - ci[None] * p1r[:, :, None, :]

    S, Gs = N_SLABS, GROUPS_PER_SLAB
    c_toe = toe.reshape(T, S, LANES, LANES).transpose(1, 0, 2, 3).reshape(S, SLAB_W, LANES)
    win = jnp.concatenate([win_r, win_i], axis=2)
    c_win = win.reshape(T, S, Gs, 2 * SSM_STATE, SSM_GROUP).transpose(1, 0, 2, 4, 3).reshape(S, SLAB_W, LANES)
    cw1 = jnp.concatenate([c_toe, c_win], axis=-1).astype(BF16)
    to_out = lambda w: w.reshape(T, S, Gs, SSM_GROUP, SSM_STATE).transpose(1, 2, 4, 0, 3).reshape(S, SLAB_STATE, LANES)
    cwout = jnp.concatenate([to_out(out_r), to_out(out_i)], axis=1).astype(BF16)
    a_chunk = (pr[T].reshape(S, SLAB_STATE), pi[T].reshape(S, SLAB_STATE))
    return cw1, cwout, a_chunk


def _expansion_matrix():
    e = np.zeros((2 * LANES, SLAB_W + 2 * SLAB_STATE), np.float32)
    for h in range(GROUPS_PER_SLAB):
        for t in range(S5_CHUNK):
            for d in range(SSM_GROUP):
                e[t * SSM_GROUP + d, t * LANES + h * SSM_GROUP + d] = 1.0
        for p in range(SSM_STATE):
            e[LANES + p, SLAB_W + h * SSM_STATE + p] = 1.0
            e[LANES + SSM_STATE + p, SLAB_W + SLAB_STATE + h * SSM_STATE + p] = 1.0
    return jnp.asarray(e, BF16)


def _scan_powers(a_chunk):
    ar, ai = a_chunk
    r, i = ar, ai
    steps, tile_r, tile_i = [], [], []
    in_tile = jnp.arange(SUBLANES)[None, :, None]
    for s in range(SCAN_STEPS):
        on = in_tile >= (1 << s)
        steps.append(jnp.stack([jnp.where(on, r[:, None, :], 0.0), jnp.where(on, i[:, None, :], 0.0)], axis=1))
        r, i = r * r - i * i, 2.0 * r * i
    r, i = ar, ai
    for _ in range(SUBLANES):
        tile_r.append(r)
        tile_i.append(i)
        r, i = r * ar - i * ai, r * ai + i * ar
    return jnp.stack(steps, axis=1), jnp.stack([jnp.stack(tile_r, axis=1), jnp.stack(tile_i, axis=1)], axis=1)


def _s5_pre_kernel(x_ref, g_ref, u_ref):
    nb = u_ref.shape[1]
    u = _rms(x_ref[...], g_ref[...])
    uj = jnp.swapaxes(u.reshape(nb, S5_CHUNK, D_MODEL), 0, 1)
    for j in range(S5_CHUNK):
        for s in range(N_SLABS):
            u_ref[s, :, j * LANES:(j + 1) * LANES] = uj[j, :, s * LANES:(s + 1) * LANES].astype(BF16)


def _expand_block_diag(c, e_ref, w_ref, row_period, row_group):
    n_rows = c.shape[0]
    tile = SLAB_STATE
    for ct in range(w_ref.shape[1] // tile):
        x = jnp.dot(c, e_ref[:c.shape[1], ct * tile:(ct + 1) * tile], preferred_element_type=F32)
        row = lax.broadcasted_iota(jnp.int32, (n_rows, tile), 0)
        col = lax.broadcasted_iota(jnp.int32, (n_rows, tile), 1)
        if ct * tile < SLAB_W:
            col_g = (col % LANES) // SSM_GROUP
        else:
            col_g = col // SSM_STATE
        keep = (row % row_period) // row_group == col_g
        w_ref[:, ct * tile:(ct + 1) * tile] = jnp.where(keep, x, 0.0).astype(w_ref.dtype)


def _s5_core_kernel(u_ref, cw1_ref, cwout_ref, e_ref, stp_ref, ptab_ref, y_ref, w1_ref, wout_ref, ra_ref, rb_ref,
                    *, nb):
    _expand_block_diag(cw1_ref[0], e_ref, w1_ref, LANES, SSM_GROUP)
    _expand_block_diag(cwout_ref[0], e_ref, wout_ref, SLAB_STATE, SSM_STATE)
    n_blocks = u_ref.shape[1] // nb
    n_tiles = nb // SUBLANES
    tile = (SUBLANES, SLAB_STATE)
    in_tile = lax.broadcasted_iota(jnp.int32, tile, 0)
    pt_r, pt_i = ptab_ref[0, 0], ptab_ref[0, 1]

    def inject(b, r_ref):
        r_ref[...] = jnp.dot(u_ref[0, b * nb:(b + 1) * nb, :], w1_ref[...], preferred_element_type=F32)

    inject(0, ra_ref)
    c_r = jnp.zeros((1, SLAB_STATE), F32)
    c_i = jnp.zeros((1, SLAB_STATE), F32)
    for b in range(n_blocks):
        cur, nxt = (ra_ref, rb_ref) if b % 2 == 0 else (rb_ref, ra_ref)
        if b + 1 < n_blocks:
            inject(b + 1, nxt)
        er = cur[:, SLAB_W:SLAB_W + SLAB_STATE].reshape((n_tiles,) + tile)
        ei = cur[:, SLAB_W + SLAB_STATE:].reshape((n_tiles,) + tile)
        for i in range(SCAN_STEPS):
            p_r, p_i = stp_ref[0, i, 0], stp_ref[0, i, 1]
            sr, si = pltpu.roll(er, 1 << i, 1), pltpu.roll(ei, 1 << i, 1)
            er, ei = er + p_r * sr - p_i * si, ei + p_r * si + p_i * sr
        pr, pi = [], []
        for k in range(n_tiles):
            b_r = jnp.broadcast_to(c_r, tile)
            b_i = jnp.broadcast_to(c_i, tile)
            t_r = er[k] + pt_r * b_r - pt_i * b_i
            t_i = ei[k] + pt_r * b_i + pt_i * b_r
            pr.append(jnp.where(in_tile == 0, b_r, pltpu.roll(t_r, 1, 0)))
            pi.append(jnp.where(in_tile == 0, b_i, pltpu.roll(t_i, 1, 0)))
            c_r, c_i = t_r[SUBLANES - 1:], t_i[SUBLANES - 1:]
        prev = jnp.concatenate([jnp.concatenate(pr, axis=0), jnp.concatenate(pi, axis=0)], axis=1).astype(BF16)
        y_ref[0, b * nb:(b + 1) * nb, :] = cur[:, :SLAB_W] + jnp.dot(prev, wout_ref[...],
                                                                      preferred_element_type=F32)


def _s5_post_kernel(x_ref, y_ref, g_ref, d_ref, wglu_ref, o_ref):
    nb = y_ref.shape[1] // 2
    for half in range(2):
        rs = slice(half * nb, (half + 1) * nb)
        ps = slice(half * S5_CHUNK * nb, (half + 1) * S5_CHUNK * nb)
        yj = jnp.stack([jnp.concatenate([y_ref[s, rs, j * LANES:(j + 1) * LANES] for s in range(N_SLABS)], axis=1)
                        for j in range(S5_CHUNK)], axis=0)
        y = jnp.swapaxes(yj, 0, 1).reshape(S5_CHUNK * nb, D_MODEL)
        x = x_ref[ps, :]
        z = jax.nn.gelu(y + d_ref[...] * _rms(x, g_ref[...]), approximate=True).astype(BF16)
        zz = jnp.dot(z, wglu_ref[...], preferred_element_type=F32)
        za, zb = zz[:, :D_MODEL], zz[:, D_MODEL:]
        o_ref[ps, :] = x + za * (1.0 / (1.0 + jnp.exp(-zb)))


def _s5_layer(x, g_mix, cw1, cwout, stp, ptab, d_skip, w_glu):
    n_rows = x.shape[0] // S5_CHUNK
    nb = min(128, n_rows)
    nbp = min(128, n_rows)
    nbc = min(256, n_rows)
    assert n_rows % nb == 0 and n_rows % nbc == 0 and nbp % 16 == 0

    u = pl.pallas_call(
        _s5_pre_kernel,
        grid=(n_rows // nb,),
        in_specs=[pl.BlockSpec((S5_CHUNK * nb, D_MODEL), lambda i: (i, 0)),
                  pl.BlockSpec((1, D_MODEL), lambda i: (0, 0))],
        out_specs=pl.BlockSpec((N_SLABS, nb, SLAB_W), lambda i: (0, i, 0)),
        out_shape=jax.ShapeDtypeStruct((N_SLABS, n_rows, SLAB_W), BF16),
        compiler_params=_cparams(("parallel",)),
        name="s5_pre",
    )(x, g_mix)

    y = pl.pallas_call(
        functools.partial(_s5_core_kernel, nb=nbc),
        grid=(N_SLABS,),
        in_specs=[pl.BlockSpec((1, n_rows, SLAB_W), lambda s: (s, 0, 0)),
                  pl.BlockSpec((1, SLAB_W, 2 * LANES), lambda s: (s, 0, 0)),
                  pl.BlockSpec((1, 2 * SLAB_STATE, LANES), lambda s: (s, 0, 0)),
                  pl.BlockSpec((2 * LANES, SLAB_W + 2 * SLAB_STATE), lambda s: (0, 0)),
                  pl.BlockSpec((1,) + stp.shape[1:], lambda s: (s, 0, 0, 0, 0)),
                  pl.BlockSpec((1,) + ptab.shape[1:], lambda s: (s, 0, 0, 0))],
        out_specs=pl.BlockSpec((1, n_rows, SLAB_W), lambda s: (s, 0, 0)),
        out_shape=jax.ShapeDtypeStruct((N_SLABS, n_rows, SLAB_W), F32),
        scratch_shapes=[pltpu.VMEM((SLAB_W, SLAB_W + 2 * SLAB_STATE), BF16),
                        pltpu.VMEM((2 * SLAB_STATE, SLAB_W), BF16),
                        pltpu.VMEM((nbc, SLAB_W + 2 * SLAB_STATE), F32),
                        pltpu.VMEM((nbc, SLAB_W + 2 * SLAB_STATE), F32)],
        compiler_params=_cparams(("parallel",)),
        name="s5_core",
    )(u, cw1, cwout, _expansion_matrix(), stp, ptab)

    return pl.pallas_call(
        _s5_post_kernel,
        grid=(n_rows // nbp,),
        in_specs=[pl.BlockSpec((S5_CHUNK * nbp, D_MODEL), lambda i: (i, 0)),
                  pl.BlockSpec((N_SLABS, nbp, SLAB_W), lambda i: (0, i, 0)),
                  _const_spec((1, D_MODEL)),
                  _const_spec((1, D_MODEL)),
                  _const_spec((D_MODEL, 2 * D_MODEL))],
        out_specs=pl.BlockSpec((S5_CHUNK * nbp, D_MODEL), lambda i: (i, 0)),
        out_shape=jax.ShapeDtypeStruct(x.shape, F32),
        compiler_params=_cparams(("parallel",)),
        name="s5_post",
    )(x, y, g_mix, d_skip, w_glu)


def _ffn_body(h, g, w1_ref, w3_ref, w2_ref):
    hn = _rms(h, g).astype(BF16)
    acc = jnp.zeros(h.shape, F32)
    for f in range(FFN_HIDDEN // FFN_TILE):
        sl = slice(f * FFN_TILE, (f + 1) * FFN_TILE)
        a = jnp.dot(hn, w1_ref[:, sl], preferred_element_type=F32)
        b = jnp.dot(hn, w3_ref[:, sl], preferred_element_type=F32)
        t = (a * (1.0 / (1.0 + jnp.exp(-a))) * b).astype(BF16)
        acc = acc + jnp.dot(t, w2_ref[sl, :], preferred_element_type=F32)
    return h + acc


def _ffn_kernel(h_ref, g_ref, w1_ref, w3_ref, w2_ref, o_ref):
    o_ref[...] = _ffn_body(h_ref[...], g_ref[...], w1_ref, w3_ref, w2_ref)


def _attn_out_ffn_kernel(h_ref, a_ref, wo_ref, g_ref, w1_ref, w3_ref, w2_ref, gf_ref, o_ref):
    h = h_ref[...] + jnp.dot(a_ref[...], wo_ref[...], preferred_element_type=F32)
    o_ref[...] = _rms(_ffn_body(h, g_ref[...], w1_ref, w3_ref, w2_ref), gf_ref[...])


def _const_spec(shape):
    return pl.BlockSpec(shape, lambda i: (0,) * len(shape), pipeline_mode=pl.Buffered(1))


def _layer_spec(w, layer):
    return pl.BlockSpec((None,) + w.shape[1:], lambda i: (layer, 0, 0), pipeline_mode=pl.Buffered(1))


def _ffn_layer(h, g, w1, w3, w2, layer):
    n, tm = h.shape[0], min(512, h.shape[0])
    return pl.pallas_call(
        _ffn_kernel,
        grid=(n // tm,),
        in_specs=[pl.BlockSpec((tm, D_MODEL), lambda i: (i, 0)),
                  _const_spec((1, D_MODEL)),
                  _layer_spec(w1, layer), _layer_spec(w3, layer), _layer_spec(w2, layer)],
        out_specs=pl.BlockSpec((tm, D_MODEL), lambda i: (i, 0)),
        out_shape=jax.ShapeDtypeStruct(h.shape, F32),
        compiler_params=_cparams(("parallel",)),
        name="ffn",
    )(h, g, w1, w3, w2)


def _attn_out_ffn_layer(h, a, wo, g, w1, w3, w2, gf, layer):
    n, tm = h.shape[0], min(512, h.shape[0])
    return pl.pallas_call(
        _attn_out_ffn_kernel,
        grid=(n // tm,),
        in_specs=[pl.BlockSpec((tm, D_MODEL), lambda i: (i, 0)),
                  pl.BlockSpec((tm, D_MODEL), lambda i: (i, 0)),
                  _const_spec((D_MODEL, D_MODEL)),
                  _const_spec((1, D_MODEL)),
                  _layer_spec(w1, layer), _layer_spec(w3, layer), _layer_spec(w2, layer),
                  _const_spec((1, D_MODEL))],
        out_specs=pl.BlockSpec((tm, D_MODEL), lambda i: (i, 0)),
        out_shape=jax.ShapeDtypeStruct(h.shape, F32),
        compiler_params=_cparams(("parallel",)),
        name="attn_out_ffn",
    )(h, a, wo, g, w1, w3, w2, gf)


def _rope(t, cos, sin_lo, sin_hi):
    return t * cos + pltpu.roll(t, LANES - ROT_DIM // 2, 1) * sin_lo + pltpu.roll(t, ROT_DIM // 2, 1) * sin_hi


def _pair_heads(x, hd, lane):
    hk = N_HEADS * HEAD_DIM
    c0 = (hd // 2) * LANES
    t1, t2 = x[:, c0:c0 + LANES], x[:, hk + c0:hk + c0 + LANES]
    if hd % 2 == 0:
        return jnp.where(lane < HEAD_DIM, t1, pltpu.roll(t2, HEAD_DIM, 1))
    return jnp.where(lane < HEAD_DIM, pltpu.roll(t1, HEAD_DIM, 1), t2)


def _proj_kernel(h_ref, gkv_ref, gq_ref, wkv_ref, wq_ref, cs_ref, sel_ref, one_ref, kk_ref, v_ref, qq_ref):
    tm = h_ref.shape[0] // 2
    lane = lax.broadcasted_iota(jnp.int32, (tm, LANES), 1)
    for half in range(2):
        rs = slice(half * tm, (half + 1) * tm)
        h = h_ref[rs, :]
        tab = jnp.dot(cs_ref[rs, :], sel_ref[...], preferred_element_type=F32)
        cos, slo, shi = tab[:, :LANES] + one_ref[...], tab[:, LANES:2 * LANES], tab[:, 2 * LANES:]
        kv = jnp.dot(_rms(h, gkv_ref[...]).astype(BF16), wkv_ref[...], preferred_element_type=F32)
        q = jnp.dot(_rms(h, gq_ref[...]).astype(BF16), wq_ref[...], preferred_element_type=F32)
        for hd in range(N_HEADS):
            kk_ref[hd, rs, :] = _rope(_pair_heads(kv, hd, lane), cos, slo, shi).astype(BF16)
            qq_ref[hd, rs, :] = (_rope(_pair_heads(q, hd, lane), cos, slo, shi) * Q_SCALE).astype(BF16)
            v_ref[hd, rs, :] = kv[:, D_MODEL + hd * LANES:D_MODEL + (hd + 1) * LANES].astype(BF16)


def _proj_layer(h, gkv, gq, wkv, wq, cs, sel, one):
    n, tm = h.shape[0], min(1024, h.shape[0])
    hshape = jax.ShapeDtypeStruct((N_HEADS, n, LANES), BF16)
    hspec = pl.BlockSpec((N_HEADS, tm, LANES), lambda i: (0, i, 0))
    return pl.pallas_call(
        _proj_kernel,
        grid=(n // tm,),
        in_specs=[pl.BlockSpec((tm, D_MODEL), lambda i: (i, 0)),
                  _const_spec((1, D_MODEL)), _const_spec((1, D_MODEL)),
                  _const_spec((D_MODEL, 2 * D_MODEL)), _const_spec((D_MODEL, D_MODEL)),
                  pl.BlockSpec((tm, cs.shape[1]), lambda i: (i, 0)),
                  _const_spec(sel.shape), _const_spec((1, LANES))],
        out_specs=[hspec, hspec, hspec],
        out_shape=[hshape, hshape, hshape],
        compiler_params=_cparams(("parallel",)),
        name="qkv_proj",
    )(h, gkv, gq, wkv, wq, cs, sel, one)


def _rope_tables(n):
    half = ROT_DIM // 2
    inv_freq = ROPE_THETA ** (-jnp.arange(half, dtype=F32) * 2.0 / ROT_DIM)
    ang = inv_freq[:, None] * jnp.arange(n, dtype=jnp.int32).astype(F32)[None, :]
    cs = jnp.concatenate([jnp.cos(ang), jnp.sin(ang)], axis=0).T
    hi = cs.astype(BF16)
    lo = (cs - hi.astype(F32)).astype(BF16)
    sel = np.zeros((ROT_DIM, 3 * LANES), np.float32)
    one = np.ones((1, LANES), np.float32)
    for l in range(LANES):
        d = l % HEAD_DIM
        if d < ROT_DIM:
            one[0, l] = 0.0
            sel[d % half, l] = 1.0
            if d < half:
                sel[half + d, LANES + l] = -1.0
            else:
                sel[half + d - half, 2 * LANES + l] = 1.0
    sel = np.concatenate([sel, sel], axis=0)
    return jnp.concatenate([hi, lo], axis=1), jnp.asarray(sel, BF16), jnp.asarray(one)


def _attn_kernel(lam_ref, qq_ref, kk_ref, v_ref, g_ref, o_ref, q2_ref, sa_ref, sb_ref, ma_ref, mb_ref,
                 ala_ref, alb_ref, acco_ref, accl_ref, *, tq, tk, lam_init):
    qi = pl.program_id(1)
    lane = lax.broadcasted_iota(jnp.int32, (tk, LANES), 1)
    zero = jnp.zeros((tk, LANES), BF16)
    for half in range(2):
        q = qq_ref[0, half * tk:(half + 1) * tk, :]
        q2_ref[(2 * half) * tk:(2 * half + 1) * tk, :] = jnp.where(lane < HEAD_DIM, q, zero)
        q2_ref[(2 * half + 1) * tk:(2 * half + 2) * tk, :] = jnp.where(lane >= HEAD_DIM, q, zero)
    mb_ref[...] = jnp.full(mb_ref.shape, -jnp.inf, F32)
    acco_ref[...] = jnp.zeros(acco_ref.shape, F32)
    accl_ref[...] = jnp.zeros(accl_ref.shape, F32)
    buf_a = (sa_ref, ma_ref, ala_ref)
    buf_b = (sb_ref, mb_ref, alb_ref)
    late = slice(2 * tk, 4 * tk)

    def causal(s):
        r = lax.broadcasted_iota(jnp.int32, s.shape, 0)
        c = lax.broadcasted_iota(jnp.int32, s.shape, 1)
        return jnp.where((c // CHUNK) <= ((r % tk) // CHUNK), s, -jnp.inf)

    def stats(s, rows, buf):
        s_ref, m_ref, al_ref = buf
        s_ref[rows, :] = s
        m_prev = (mb_ref if buf is buf_a else ma_ref)[rows, :]
        m_new = jnp.maximum(m_prev, jnp.max(s, axis=1, keepdims=True))
        al_ref[rows, :] = jnp.exp2(m_prev - m_new)
        m_ref[rows, :] = m_new

    def scores(blk, buf, kind):
        k = kk_ref[0, pl.ds(pl.multiple_of(blk * tk, tk), tk), :]
        dims = (((1,), (1,)), ((), ()))
        if kind == "last":
            stats(causal(lax.dot_general(q2_ref[late, :], k, dims, preferred_element_type=F32)), late, buf)
            return
        s = lax.dot_general(q2_ref[...], k, dims, preferred_element_type=F32)
        if kind == "diag":
            stats(causal(s[:2 * tk]), slice(0, 2 * tk), buf)
            stats(s[2 * tk:], late, buf)
        else:
            stats(s, slice(None), buf)

    def accumulate(blk, buf, rows=slice(None)):
        s_ref, m_ref, al_ref = buf
        v = v_ref[0, pl.ds(pl.multiple_of(blk * tk, tk), tk), :]
        v1 = jnp.concatenate([v, jnp.ones_like(v)], axis=1)
        m = m_ref[rows, :]
        p = jnp.concatenate([jnp.exp2(s_ref[rows, j * LANES:(j + 1) * LANES] - m) for j in range(tk // LANES)],
                            axis=1).astype(BF16)
        pv = jnp.dot(p, v1, preferred_element_type=F32)
        al = al_ref[rows, :]
        acco_ref[rows, :] = al * acco_ref[rows, :] + pv[:, :LANES]
        accl_ref[rows, :] = al * accl_ref[rows, :] + pv[:, LANES:]

    @pl.when(qi == 0)
    def _():
        scores(0, buf_a, "diag")

    @pl.when(qi > 0)
    def _():
        scores(0, buf_a, "full")

    def pair(t, next_kind):
        scores(2 * t + 1, buf_b, "full")
        accumulate(2 * t, buf_a)
        scores(2 * t + 2, buf_a, next_kind)
        accumulate(2 * t + 1, buf_b)

    n_loop = jnp.maximum(qi - 1, 0)
    odd = n_loop % 2

    @pl.when(odd == 1)
    def _():
        pair(0, "full")

    def body(u, c):
        t = odd + 2 * u
        pair(t, "full")
        pair(t + 1, "full")
        return c

    lax.fori_loop(0, n_loop // 2, body, 0)

    @pl.when(qi > 0)
    def _():
        pair(qi - 1, "diag")

    scores(2 * qi + 1, buf_b, "last")
    accumulate(2 * qi, buf_a)
    accumulate(2 * qi + 1, buf_b, late)

    o = acco_ref[...] / accl_ref[...]
    lam = lam_ref[0]
    o = jnp.concatenate([o[:tk] - lam * o[tk:2 * tk], o[2 * tk:3 * tk] - lam * o[3 * tk:]], axis=0)
    o_ref[...] = (_rms(o, g_ref[...]) * (1.0 - lam_init)).astype(o_ref.dtype)


def _attn_layer(lam, qq, kk, v, g, lam_init):
    n = qq.shape[1]
    tq = min(1024, n)
    tk = tq // 2
    assert n % tq == 0 and tk % CHUNK == 0 and tk % LANES == 0
    stat = pltpu.VMEM((2 * tq, LANES), F32)
    return pl.pallas_call(
        functools.partial(_attn_kernel, tq=tq, tk=tk, lam_init=lam_init),
        grid=(N_HEADS, n // tq),
        in_specs=[pl.BlockSpec(memory_space=pltpu.SMEM),
                  pl.BlockSpec((1, tq, LANES), lambda h, i: (h, i, 0)),
                  pl.BlockSpec((1, n, LANES), lambda h, i: (h, 0, 0)),
                  pl.BlockSpec((1, n, LANES), lambda h, i: (h, 0, 0)),
                  pl.BlockSpec((1, LANES), lambda h, i: (0, 0))],
        out_specs=pl.BlockSpec((tq, LANES), lambda h, i: (i, h)),
        out_shape=jax.ShapeDtypeStruct((n, N_HEADS * V_DIM), BF16),
        scratch_shapes=[pltpu.VMEM((2 * tq, LANES), BF16),
                        pltpu.VMEM((2 * tq, tk), F32), pltpu.VMEM((2 * tq, tk), F32),
                        stat, stat, stat, stat, stat, stat],
        compiler_params=_cparams(("parallel", "arbitrary")),
        name="diff_attn",
    )(lam, qq, kk, v, g)


def kernel(x, norm_mix_g, norm_ffn_g, ffn_w1, ffn_w3, ffn_w2, ssm_lam_re, ssm_lam_im, ssm_log_dt, ssm_b_re, ssm_b_im, ssm_c_re, ssm_c_im, ssm_d, ssm_w_glu, kv_norm_g, w_kv, attn_w_q, attn_lq1, attn_lk1, attn_lq2, attn_lk2, attn_subln_g, attn_w_o, final_norm_g):
    bsz, L, _ = x.shape
    assert bsz == 1 and L % (S5_CHUNK * 8) == 0
    row = lambda t: t.reshape(1, -1).astype(F32)

    cw1, cwout, a_chunk = _s5_weights(ssm_lam_re[0], ssm_lam_im[0], ssm_log_dt[0], ssm_b_re[0], ssm_b_im[0],
                                      ssm_c_re[0], ssm_c_im[0])
    stp, ptab = _scan_powers(a_chunk)
    h = _s5_layer(x.reshape(L, D_MODEL), row(norm_mix_g[0]), cw1, cwout, stp, ptab, row(ssm_d[0]),
                  ssm_w_glu[0].astype(BF16))
    fw1, fw3, fw2 = ffn_w1.astype(BF16), ffn_w3.astype(BF16), ffn_w2.astype(BF16)
    h = _ffn_layer(h, row(norm_ffn_g[0]), fw1, fw3, fw2, 0)

    cs, sel, one = _rope_tables(L)
    kk, v, qq = _proj_layer(h, row(kv_norm_g), row(norm_mix_g[1]), w_kv.astype(BF16), attn_w_q[0].astype(BF16),
                            cs, sel, one)
    lam_init = 0.8 - 0.6 * math.exp(-0.3 * N_A_LAYERS)
    lam = (jnp.exp(jnp.sum(attn_lq1[0].astype(F32) * attn_lk1[0].astype(F32)))
           - jnp.exp(jnp.sum(attn_lq2[0].astype(F32) * attn_lk2[0].astype(F32))) + lam_init).reshape(1)
    a = _attn_layer(lam, qq, kk, v, row(attn_subln_g[0]), lam_init)
    out = _attn_out_ffn_layer(h, a, attn_w_o[0].astype(BF16), row(norm_ffn_g[1]), fw1, fw3, fw2,
                              row(final_norm_g), 1)
    return out.reshape(1, L, D_MODEL)
```

```python
import functools
import math

import jax
import jax.numpy as jnp
import numpy as np
from jax import lax
from jax.experimental import pallas as pl
from jax.experimental.pallas import tpu as pltpu

F32 = jnp.float32
BF16 = jnp.bfloat16

D_MODEL = 1024
CHUNK = 64
SSM_GROUP = 16
N_GROUPS = D_MODEL // SSM_GROUP
SSM_STATE = 64
N_HEADS = 8
HEAD_DIM = 64
V_DIM = 2 * HEAD_DIM
ROT_DIM = HEAD_DIM // 4
ROPE_THETA = 500000.0
FFN_HIDDEN = 2816
EPS = 1e-6
N_A_LAYERS = 1

LANES = 128
SUBLANES = 8
S5_CHUNK = 8
N_SLABS = D_MODEL // LANES
GROUPS_PER_SLAB = LANES // SSM_GROUP
SLAB_STATE = GROUPS_PER_SLAB * SSM_STATE
SLAB_W = S5_CHUNK * LANES
SCAN_STEPS = 3
FFN_TILE = 256
Q_SCALE = HEAD_DIM ** -0.5 * math.log2(math.e)
VMEM_LIMIT = 56 * 1024 * 1024


def _cparams(sem):
    return pltpu.CompilerParams(dimension_semantics=sem, vmem_limit_bytes=VMEM_LIMIT)


def _rms(x, g):
    return x * lax.rsqrt(jnp.mean(x * x, axis=-1, keepdims=True) + EPS) * g


def _s5_weights(lam_re, lam_im, log_dt, b_re, b_im, c_re, c_im):
    hi = lax.Precision.HIGHEST
    lr, li = lam_re.astype(F32), lam_im.astype(F32)
    dt = jnp.exp(log_dt.astype(F32))[:, None]
    mag = jnp.exp(lr * dt)
    ar, ai = mag * jnp.cos(li * dt), mag * jnp.sin(li * dt)
    nr, ni = ar - 1.0, ai
    den = lr * lr + li * li
    fr = (nr * lr + ni * li) / den
    fi = (ni * lr - nr * li) / den
    br, bi = b_re.astype(F32), b_im.astype(F32)
    bbr = fr[..., None] * br - fi[..., None] * bi
    bbi = fr[..., None] * bi + fi[..., None] * br
    cr, ci = c_re.astype(F32), c_im.astype(F32)

    prs, pis = [jnp.ones_like(ar)], [jnp.zeros_like(ar)]
    for _ in range(S5_CHUNK):
        prs.append(prs[-1] * ar - pis[-1] * ai)
        pis.append(prs[-2] * ai + pis[-1] * ar)
    pr = jnp.stack(prs)
    pi = jnp.stack(pis)

    T = S5_CHUNK
    kr = pr[:T, :, :, None] * bbr[None] - pi[:T, :, :, None] * bbi[None]
    ki = pr[:T, :, :, None] * bbi[None] + pi[:T, :, :, None] * bbr[None]
    klag = (jnp.einsum('gdp,kgpc->gckd', cr, kr, precision=hi)
            - jnp.einsum('gdp,kgpc->gckd', ci, ki, precision=hi)).reshape(N_GROUPS, SSM_GROUP, LANES)
    toe = jnp.stack([jnp.pad(klag, ((0, 0), (0, 0), (SSM_GROUP * j, 0)))[..., :LANES] for j in range(T)])
    win_r = kr[::-1]
    win_i = ki[::-1]
    p1r, p1i = pr[1:], pi[1:]
    out_r = cr[None] * p1r[:, :, None, :] - ci[None] * p1i[:, :, None, :]
    out_i = -cr[None] * p1i[:, :, None, :] - ci[None] * p1r[:, :, None, :]

    S, Gs = N_SLABS, GROUPS_PER_SLAB
    c_toe = toe.reshape(T, S, LANES, LANES).transpose(1, 0, 2, 3).reshape(S, SLAB_W, LANES)
    win = jnp.concatenate([win_r, win_i], axis=2)
    c_win = win.reshape(T, S, Gs, 2 * SSM_STATE, SSM_GROUP).transpose(1, 0, 2, 4, 3).reshape(S, SLAB_W, LANES)
    cw1 = jnp.concatenate([c_toe, c_win], axis=-1).astype(BF16)
    to_out = lambda w: w.reshape(T, S, Gs, SSM_GROUP, SSM_STATE).transpose(1, 2, 4, 0, 3).reshape(S, SLAB_STATE, LANES)
    cwout = jnp.concatenate([to_out(out_r), to_out(out_i)], axis=1).astype(BF16)
    a_chunk = (pr[T].reshape(S, SLAB_STATE), pi[T].reshape(S, SLAB_STATE))
    return cw1, cwout, a_chunk


def _expansion_matrix():
    e = np.zeros((2 * LANES, SLAB_W + 2 * SLAB_STATE), np.float32)
    for h in range(GROUPS_PER_SLAB):
        for t in range(S5_CHUNK):
            for d in range(SSM_GROUP):
                e[t * SSM_GROUP + d, t * LANES + h * SSM_GROUP + d] = 1.0
        for p in range(SSM_STATE):
            e[LANES + p, SLAB_W + h * SSM_STATE + p] = 1.0
            e[LANES + SSM_STATE + p, SLAB_W + SLAB_STATE + h * SSM_STATE + p] = 1.0
    return jnp.asarray(e, BF16)


def _scan_powers(a_chunk):
    ar, ai = a_chunk
    r, i = ar, ai
    steps, tile_r, tile_i = [], [], []
    in_tile = jnp.arange(SUBLANES)[None, :, None]
    for s in range(SCAN_STEPS):
        on = in_tile >= (1 << s)
        steps.append(jnp.stack([jnp.where(on, r[:, None, :], 0.0), jnp.where(on, i[:, None, :], 0.0)], axis=1))
        r, i = r * r - i * i, 2.0 * r * i
    r, i = ar, ai
    for _ in range(SUBLANES):
        tile_r.append(r)
        tile_i.append(i)
        r, i = r * ar - i * ai, r * ai + i * ar
    return jnp.stack(steps, axis=1), jnp.stack([jnp.stack(tile_r, axis=1), jnp.stack(tile_i, axis=1)], axis=1)


def _s5_pre_kernel(x_ref, g_ref, u_ref):
    nb = u_ref.shape[1]
    u = _rms(x_ref[...], g_ref[...])
    uj = jnp.swapaxes(u.reshape(nb, S5_CHUNK, D_MODEL), 0, 1)
    for j in range(S5_CHUNK):
        for s in range(N_SLABS):
            u_ref[s, :, j * LANES:(j + 1) * LANES] = uj[j, :, s * LANES:(s + 1) * LANES].astype(BF16)


def _expand_block_diag(c, e_ref, w_ref, row_period, row_group):
    n_rows = c.shape[0]
    tile = SLAB_STATE
    for ct in range(w_ref.shape[1] // tile):
        x = jnp.dot(c, e_ref[:c.shape[1], ct * tile:(ct + 1) * tile], preferred_element_type=F32)
        row = lax.broadcasted_iota(jnp.int32, (n_rows, tile), 0)
        col = lax.broadcasted_iota(jnp.int32, (n_rows, tile), 1)
        if ct * tile < SLAB_W:
            col_g = (col % LANES) // SSM_GROUP
        else:
            col_g = col // SSM_STATE
        keep = (row % row_period) // row_group == col_g
        w_ref[:, ct * tile:(ct + 1) * tile] = jnp.where(keep, x, 0.0).astype(w_ref.dtype)


def _s5_core_kernel(u_ref, cw1_ref, cwout_ref, e_ref, stp_ref, ptab_ref, y_ref, w1_ref, wout_ref, ra_ref, rb_ref,
                    *, nb):
    _expand_block_diag(cw1_ref[0], e_ref, w1_ref, LANES, SSM_GROUP)
    _expand_block_diag(cwout_ref[0], e_ref, wout_ref, SLAB_STATE, SSM_STATE)
    n_blocks = u_ref.shape[1] // nb
    n_tiles = nb // SUBLANES
    tile = (SUBLANES, SLAB_STATE)
    in_tile = lax.broadcasted_iota(jnp.int32, tile, 0)
    pt_r, pt_i = ptab_ref[0, 0], ptab_ref[0, 1]

    def inject(b, r_ref):
        r_ref[...] = jnp.dot(u_ref[0, b * nb:(b + 1) * nb, :], w1_ref[...], preferred_element_type=F32)

    inject(0, ra_ref)
    c_r = jnp.zeros((1, SLAB_STATE), F32)
    c_i = jnp.zeros((1, SLAB_STATE), F32)
    for b in range(n_blocks):
        cur, nxt = (ra_ref, rb_ref) if b % 2 == 0 else (rb_ref, ra_ref)
        if b + 1 < n_blocks:
            inject(b + 1, nxt)
        er = cur[:, SLAB_W:SLAB_W + SLAB_STATE].reshape((n_tiles,) + tile)
        ei = cur[:, SLAB_W + SLAB_STATE:].reshape((n_tiles,) + tile)
        for i in range(SCAN_STEPS):
            p_r, p_i = stp_ref[0, i, 0], stp_ref[0, i, 1]
            sr, si = pltpu.roll(er, 1 << i, 1), pltpu.roll(ei, 1 << i, 1)
            er, ei = er + p_r * sr - p_i * si, ei + p_r * si + p_i * sr
        pr, pi = [], []
        for k in range(n_tiles):
            b_r = jnp.broadcast_to(c_r, tile)
            b_i = jnp.broadcast_to(c_i, tile)
            t_r = er[k] + pt_r * b_r - pt_i * b_i
            t_i = ei[k] + pt_r * b_i + pt_i * b_r
            pr.append(jnp.where(in_tile == 0, b_r, pltpu.roll(t_r, 1, 0)))
            pi.append(jnp.where(in_tile == 0, b_i, pltpu.roll(t_i, 1, 0)))
            c_r, c_i = t_r[SUBLANES - 1:], t_i[SUBLANES - 1:]
        prev = jnp.concatenate([jnp.concatenate(pr, axis=0), jnp.concatenate(pi, axis=0)], axis=1).astype(BF16)
        y_ref[0, b * nb:(b + 1) * nb, :] = cur[:, :SLAB_W] + jnp.dot(prev, wout_ref[...],
                                                                      preferred_element_type=F32)


def _s5_post_kernel(x_ref, y_ref, g_ref, d_ref, wglu_ref, o_ref):
    nb = y_ref.shape[1] // 2
    for half in range(2):
        rs = slice(half * nb, (half + 1) * nb)
        ps = slice(half * S5_CHUNK * nb, (half + 1) * S5_CHUNK * nb)
        yj = jnp.stack([jnp.concatenate([y_ref[s, rs, j * LANES:(j + 1) * LANES] for s in range(N_SLABS)], axis=1)
                        for j in range(S5_CHUNK)], axis=0)
        y = jnp.swapaxes(yj, 0, 1).reshape(S5_CHUNK * nb, D_MODEL)
        x = x_ref[ps, :]
        z = jax.nn.gelu(y + d_ref[...] * _rms(x, g_ref[...]), approximate=True).astype(BF16)
        zz = jnp.dot(z, wglu_ref[...], preferred_element_type=F32)
        za, zb = zz[:, :D_MODEL], zz[:, D_MODEL:]
        o_ref[ps, :] = x + za * (1.0 / (1.0 + jnp.exp(-zb)))


def _s5_layer(x, g_mix, cw1, cwout, stp, ptab, d_skip, w_glu):
    n_rows = x.shape[0] // S5_CHUNK
    nb = min(128, n_rows)
    nbp = min(128, n_rows)
    nbc = min(256, n_rows)
    assert n_rows % nb == 0 and n_rows % nbc == 0 and nbp % 16 == 0

    u = pl.pallas_call(
        _s5_pre_kernel,
        grid=(n_rows // nb,),
        in_specs=[pl.BlockSpec((S5_CHUNK * nb, D_MODEL), lambda i: (i, 0)),
                  pl.BlockSpec((1, D_MODEL), lambda i: (0, 0))],
        out_specs=pl.BlockSpec((N_SLABS, nb, SLAB_W), lambda i: (0, i, 0)),
        out_shape=jax.ShapeDtypeStruct((N_SLABS, n_rows, SLAB_W), BF16),
        compiler_params=_cparams(("parallel",)),
        name="s5_pre",
    )(x, g_mix)

    y = pl.pallas_call(
        functools.partial(_s5_core_kernel, nb=nbc),
        grid=(N_SLABS,),
        in_specs=[pl.BlockSpec((1, n_rows, SLAB_W), lambda s: (s, 0, 0)),
                  pl.BlockSpec((1, SLAB_W, 2 * LANES), lambda s: (s, 0, 0)),
                  pl.BlockSpec((1, 2 * SLAB_STATE, LANES), lambda s: (s, 0, 0)),
                  pl.BlockSpec((2 * LANES, SLAB_W + 2 * SLAB_STATE), lambda s: (0, 0)),
                  pl.BlockSpec((1,) + stp.shape[1:], lambda s: (s, 0, 0, 0, 0)),
                  pl.BlockSpec((1,) + ptab.shape[1:], lambda s: (s, 0, 0, 0))],
        out_specs=pl.BlockSpec((1, n_rows, SLAB_W), lambda s: (s, 0, 0)),
        out_shape=jax.ShapeDtypeStruct((N_SLABS, n_rows, SLAB_W), F32),
        scratch_shapes=[pltpu.VMEM((SLAB_W, SLAB_W + 2 * SLAB_STATE), BF16),
                        pltpu.VMEM((2 * SLAB_STATE, SLAB_W), BF16),
                        pltpu.VMEM((nbc, SLAB_W + 2 * SLAB_STATE), F32),
                        pltpu.VMEM((nbc, SLAB_W + 2 * SLAB_STATE), F32)],
        compiler_params=_cparams(("parallel",)),
        name="s5_core",
    )(u, cw1, cwout, _expansion_matrix(), stp, ptab)

    return pl.pallas_call(
        _s5_post_kernel,
        grid=(n_rows // nbp,),
        in_specs=[pl.BlockSpec((S5_CHUNK * nbp, D_MODEL), lambda i: (i, 0)),
                  pl.BlockSpec((N_SLABS, nbp, SLAB_W), lambda i: (0, i, 0)),
                  _const_spec((1, D_MODEL)),
                  _const_spec((1, D_MODEL)),
                  _const_spec((D_MODEL, 2 * D_MODEL))],
        out_specs=pl.BlockSpec((S5_CHUNK * nbp, D_MODEL), lambda i: (i, 0)),
        out_shape=jax.ShapeDtypeStruct(x.shape, F32),
        compiler_params=_cparams(("parallel",)),
        name="s5_post",
    )(x, y, g_mix, d_skip, w_glu)


def _ffn_body(h, g, w1_ref, w3_ref, w2_ref):
    hn = _rms(h, g).astype(BF16)
    acc = jnp.zeros(h.shape, F32)
    for f in range(FFN_HIDDEN // FFN_TILE):
        sl = slice(f * FFN_TILE, (f + 1) * FFN_TILE)
        a = jnp.dot(hn, w1_ref[:, sl].astype(BF16), preferred_element_type=F32)
        b = jnp.dot(hn, w3_ref[:, sl].astype(BF16), preferred_element_type=F32)
        t = (a * (1.0 / (1.0 + jnp.exp(-a))) * b).astype(BF16)
        acc = acc + jnp.dot(t, w2_ref[sl, :].astype(BF16), preferred_element_type=F32)
    return h + acc


def _ffn_kernel(h_ref, g_ref, w1_ref, w3_ref, w2_ref, o_ref):
    o_ref[...] = _ffn_body(h_ref[...], g_ref[...], w1_ref, w3_ref, w2_ref)


def _attn_out_ffn_kernel(h_ref, a_ref, wo_ref, g_ref, w1_ref, w3_ref, w2_ref, gf_ref, o_ref):
    h = h_ref[...] + jnp.dot(a_ref[...], wo_ref[...], preferred_element_type=F32)
    o_ref[...] = _rms(_ffn_body(h, g_ref[...], w1_ref, w3_ref, w2_ref), gf_ref[...])


def _const_spec(shape):
    return pl.BlockSpec(shape, lambda i: (0,) * len(shape), pipeline_mode=pl.Buffered(1))


def _layer_spec(w, layer):
    return pl.BlockSpec((None,) + w.shape[1:], lambda i: (layer, 0, 0), pipeline_mode=pl.Buffered(1))


def _ffn_layer(h, g, w1, w3, w2, layer):
    n, tm = h.shape[0], min(512, h.shape[0])
    return pl.pallas_call(
        _ffn_kernel,
        grid=(n // tm,),
        in_specs=[pl.BlockSpec((tm, D_MODEL), lambda i: (i, 0)),
                  _const_spec((1, D_MODEL)),
                  _layer_spec(w1, layer), _layer_spec(w3, layer), _layer_spec(w2, layer)],
        out_specs=pl.BlockSpec((tm, D_MODEL), lambda i: (i, 0)),
        out_shape=jax.ShapeDtypeStruct(h.shape, F32),
        compiler_params=_cparams(("parallel",)),
        name="ffn",
    )(h, g, w1, w3, w2)


def _attn_out_ffn_layer(h, a, wo, g, w1, w3, w2, gf, layer):
    n, tm = h.shape[0], min(512, h.shape[0])
    return pl.pallas_call(
        _attn_out_ffn_kernel,
        grid=(n // tm,),
        in_specs=[pl.BlockSpec((tm, D_MODEL), lambda i: (i, 0)),
                  pl.BlockSpec((tm, D_MODEL), lambda i: (i, 0)),
                  _const_spec((D_MODEL, D_MODEL)),
                  _const_spec((1, D_MODEL)),
                  _layer_spec(w1, layer), _layer_spec(w3, layer), _layer_spec(w2, layer),
                  _const_spec((1, D_MODEL))],
        out_specs=pl.BlockSpec((tm, D_MODEL), lambda i: (i, 0)),
        out_shape=jax.ShapeDtypeStruct(h.shape, F32),
        compiler_params=_cparams(("parallel",)),
        name="attn_out_ffn",
    )(h, a, wo, g, w1, w3, w2, gf)


def _rope(t, cos, sin_lo, sin_hi):
    return t * cos + pltpu.roll(t, LANES - ROT_DIM // 2, 1) * sin_lo + pltpu.roll(t, ROT_DIM // 2, 1) * sin_hi


def _pair_heads(x, hd, lane):
    hk = N_HEADS * HEAD_DIM
    c0 = (hd // 2) * LANES
    t1, t2 = x[:, c0:c0 + LANES], x[:, hk + c0:hk + c0 + LANES]
    if hd % 2 == 0:
        return jnp.where(lane < HEAD_DIM, t1, pltpu.roll(t2, HEAD_DIM, 1))
    return jnp.where(lane < HEAD_DIM, pltpu.roll(t1, HEAD_DIM, 1), t2)


def _proj_kernel(h_ref, gkv_ref, gq_ref, wkv_ref, wq_ref, cs_ref, sel_ref, one_ref, kk_ref, v_ref, qq_ref):
    tm = h_ref.shape[0] // 2
    lane = lax.broadcasted_iota(jnp.int32, (tm, LANES), 1)
    for half in range(2):
        rs = slice(half * tm, (half + 1) * tm)
        h = h_ref[rs, :]
        tab = jnp.dot(cs_ref[rs, :], sel_ref[...], preferred_element_type=F32)
        cos, slo, shi = tab[:, :LANES] + one_ref[...], tab[:, LANES:2 * LANES], tab[:, 2 * LANES:]
        kv = jnp.dot(_rms(h, gkv_ref[...]).astype(BF16), wkv_ref[...], preferred_element_type=F32)
        q = jnp.dot(_rms(h, gq_ref[...]).astype(BF16), wq_ref[...], preferred_element_type=F32)
        for hd in range(N_HEADS):
            kk_ref[hd, rs, :] = _rope(_pair_heads(kv, hd, lane), cos, slo, shi).astype(BF16)
            qq_ref[hd, rs, :] = (_rope(_pair_heads(q, hd, lane), cos, slo, shi) * Q_SCALE).astype(BF16)
            v_ref[hd, rs, :] = kv[:, D_MODEL + hd * LANES:D_MODEL + (hd + 1) * LANES].astype(BF16)


def _proj_layer(h, gkv, gq, wkv, wq, cs, sel, one):
    n, tm = h.shape[0], min(1024, h.shape[0])
    hshape = jax.ShapeDtypeStruct((N_HEADS, n, LANES), BF16)
    hspec = pl.BlockSpec((N_HEADS, tm, LANES), lambda i: (0, i, 0))
    return pl.pallas_call(
        _proj_kernel,
        grid=(n // tm,),
        in_specs=[pl.BlockSpec((tm, D_MODEL), lambda i: (i, 0)),
                  _const_spec((1, D_MODEL)), _const_spec((1, D_MODEL)),
                  _const_spec((D_MODEL, 2 * D_MODEL)), _const_spec((D_MODEL, D_MODEL)),
                  pl.BlockSpec((tm, cs.shape[1]), lambda i: (i, 0)),
                  _const_spec(sel.shape), _const_spec((1, LANES))],
        out_specs=[hspec, hspec, hspec],
        out_shape=[hshape, hshape, hshape],
        compiler_params=_cparams(("parallel",)),
        name="qkv_proj",
    )(h, gkv, gq, wkv, wq, cs, sel, one)


def _rope_tables(n):
    half = ROT_DIM // 2
    inv_freq = ROPE_THETA ** (-jnp.arange(half, dtype=F32) * 2.0 / ROT_DIM)
    ang = inv_freq[:, None] * jnp.arange(n, dtype=jnp.int32).astype(F32)[None, :]
    cs = jnp.concatenate([jnp.cos(ang), jnp.sin(ang)], axis=0).T
    hi = cs.astype(BF16)
    lo = (cs - hi.astype(F32)).astype(BF16)
    sel = np.zeros((ROT_DIM, 3 * LANES), np.float32)
    one = np.ones((1, LANES), np.float32)
    for l in range(LANES):
        d = l % HEAD_DIM
        if d < ROT_DIM:
            one[0, l] = 0.0
            sel[d % half, l] = 1.0
            if d < half:
                sel[half + d, LANES + l] = -1.0
            else:
                sel[half + d - half, 2 * LANES + l] = 1.0
    sel = np.concatenate([sel, sel], axis=0)
    return jnp.concatenate([hi, lo], axis=1), jnp.asarray(sel, BF16), jnp.asarray(one)


def _attn_kernel(lam_ref, qq_ref, kk_ref, v_ref, g_ref, o_ref, q2_ref, sa_ref, sb_ref, ma_ref, mb_ref,
                 ala_ref, alb_ref, acco_ref, accl_ref, *, tq, tk, lam_init):
    qi = pl.program_id(1)
    lane = lax.broadcasted_iota(jnp.int32, (tk, LANES), 1)
    zero = jnp.zeros((tk, LANES), BF16)
    for half in range(2):
        q = qq_ref[0, half * tk:(half + 1) * tk, :]
        q2_ref[(2 * half) * tk:(2 * half + 1) * tk, :] = jnp.where(lane < HEAD_DIM, q, zero)
        q2_ref[(2 * half + 1) * tk:(2 * half + 2) * tk, :] = jnp.where(lane >= HEAD_DIM, q, zero)
    mb_ref[...] = jnp.full(mb_ref.shape, -jnp.inf, F32)
    acco_ref[...] = jnp.zeros(acco_ref.shape, F32)
    accl_ref[...] = jnp.zeros(accl_ref.shape, F32)
    buf_a = (sa_ref, ma_ref, ala_ref)
    buf_b = (sb_ref, mb_ref, alb_ref)
    late = slice(2 * tk, 4 * tk)

    def causal(s):
        r = lax.broadcasted_iota(jnp.int32, s.shape, 0)
        c = lax.broadcasted_iota(jnp.int32, s.shape, 1)
        return jnp.where((c // CHUNK) <= ((r % tk) // CHUNK), s, -jnp.inf)

    def stats(s, rows, buf):
        s_ref, m_ref, al_ref = buf
        s_ref[rows, :] = s
        m_prev = (mb_ref if buf is buf_a else ma_ref)[rows, :]
        m_new = jnp.maximum(m_prev, jnp.max(s, axis=1, keepdims=True))
        al_ref[rows, :] = jnp.exp2(m_prev - m_new)
        m_ref[rows, :] = m_new

    def scores(blk, buf, kind):
        k = kk_ref[0, pl.ds(pl.multiple_of(blk * tk, tk), tk), :]
        dims = (((1,), (1,)), ((), ()))
        if kind == "last":
            stats(causal(lax.dot_general(q2_ref[late, :], k, dims, preferred_element_type=F32)), late, buf)
            return
        s = lax.dot_general(q2_ref[...], k, dims, preferred_element_type=F32)
        if kind == "diag":
            stats(causal(s[:2 * tk]), slice(0, 2 * tk), buf)
            stats(s[2 * tk:], late, buf)
        else:
            stats(s, slice(None), buf)

    def accumulate(blk, buf, rows=slice(None)):
        s_ref, m_ref, al_ref = buf
        v = v_ref[0, pl.ds(pl.multiple_of(blk * tk, tk), tk), :]
        v1 = jnp.concatenate([v, jnp.ones_like(v)], axis=1)
        m = m_ref[rows, :]
        p = jnp.concatenate([jnp.exp2(s_ref[rows, j * LANES:(j + 1) * LANES] - m) for j in range(tk // LANES)],
                            axis=1).astype(BF16)
        pv = jnp.dot(p, v1, preferred_element_type=F32)
        al = al_ref[rows, :]
        acco_ref[rows, :] = al * acco_ref[rows, :] + pv[:, :LANES]
        accl_ref[rows, :] = al * accl_ref[rows, :] + pv[:, LANES:]

    @pl.when(qi == 0)
    def _():
        scores(0, buf_a, "diag")

    @pl.when(qi > 0)
    def _():
        scores(0, buf_a, "full")

    def pair(t, next_kind):
        scores(2 * t + 1, buf_b, "full")
        accumulate(2 * t, buf_a)
        scores(2 * t + 2, buf_a, next_kind)
        accumulate(2 * t + 1, buf_b)

    n_loop = jnp.maximum(qi - 1, 0)
    odd = n_loop % 2

    @pl.when(odd == 1)
    def _():
        pair(0, "full")

    def body(u, c):
        t = odd + 2 * u
        pair(t, "full")
        pair(t + 1, "full")
        return c

    lax.fori_loop(0, n_loop // 2, body, 0)

    @pl.when(qi > 0)
    def _():
        pair(qi - 1, "diag")

    scores(2 * qi + 1, buf_b, "last")
    accumulate(2 * qi, buf_a)
    accumulate(2 * qi + 1, buf_b, late)

    o = acco_ref[...] / accl_ref[...]
    lam = lam_ref[0]
    o = jnp.concatenate([o[:tk] - lam * o[tk:2 * tk], o[2 * tk:3 * tk] - lam * o[3 * tk:]], axis=0)
    o_ref[...] = (_rms(o, g_ref[...]) * (1.0 - lam_init)).astype(o_ref.dtype)


def _attn_layer(lam, qq, kk, v, g, lam_init):
    n = qq.shape[1]
    tq = min(1024, n)
    tk = tq // 2
    assert n % tq == 0 and tk % CHUNK == 0 and tk % LANES == 0
    stat = pltpu.VMEM((2 * tq, LANES), F32)
    return pl.pallas_call(
        functools.partial(_attn_kernel, tq=tq, tk=tk, lam_init=lam_init),
        grid=(N_HEADS, n // tq),
        in_specs=[pl.BlockSpec(memory_space=pltpu.SMEM),
                  pl.BlockSpec((1, tq, LANES), lambda h, i: (h, i, 0)),
                  pl.BlockSpec((1, n, LANES), lambda h, i: (h, 0, 0)),
                  pl.BlockSpec((1, n, LANES), lambda h, i: (h, 0, 0)),
                  pl.BlockSpec((1, LANES), lambda h, i: (0, 0))],
        out_specs=pl.BlockSpec((tq, LANES), lambda h, i: (i, h)),
        out_shape=jax.ShapeDtypeStruct((n, N_HEADS * V_DIM), BF16),
        scratch_shapes=[pltpu.VMEM((2 * tq, LANES), BF16),
                        pltpu.VMEM((2 * tq, tk), F32), pltpu.VMEM((2 * tq, tk), F32),
                        stat, stat, stat, stat, stat, stat],
        compiler_params=_cparams(("parallel", "arbitrary")),
        name="diff_attn",
    )(lam, qq, kk, v, g)


def kernel(x, norm_mix_g, norm_ffn_g, ffn_w1, ffn_w3, ffn_w2, ssm_lam_re, ssm_lam_im, ssm_log_dt, ssm_b_re, ssm_b_im, ssm_c_re, ssm_c_im, ssm_d, ssm_w_glu, kv_norm_g, w_kv, attn_w_q, attn_lq1, attn_lk1, attn_lq2, attn_lk2, attn_subln_g, attn_w_o, final_norm_g):
    bsz, L, _ = x.shape
    assert bsz == 1 and L % (S5_CHUNK * 8) == 0
    row = lambda t: t.reshape(1, -1).astype(F32)

    cw1, cwout, a_chunk = _s5_weights(ssm_lam_re[0], ssm_lam_im[0], ssm_log_dt[0], ssm_b_re[0], ssm_b_im[0],
                                      ssm_c_re[0], ssm_c_im[0])
    stp, ptab = _scan_powers(a_chunk)
    h = _s5_layer(x.reshape(L, D_MODEL), row(norm_mix_g[0]), cw1, cwout, stp, ptab, row(ssm_d[0]),
                  ssm_w_glu[0].astype(BF16))
    fw1, fw3, fw2 = ffn_w1, ffn_w3, ffn_w2
    h = _ffn_layer(h, row(norm_ffn_g[0]), fw1, fw3, fw2, 0)

    cs, sel, one = _rope_tables(L)
    kk, v, qq = _proj_layer(h, row(kv_norm_g), row(norm_mix_g[1]), w_kv.astype(BF16), attn_w_q[0].astype(BF16),
                            cs, sel, one)
    lam_init = 0.8 - 0.6 * math.exp(-0.3 * N_A_LAYERS)
    lam = (jnp.exp(jnp.sum(attn_lq1[0].astype(F32) * attn_lk1[0].astype(F32)))
           - jnp.exp(jnp.sum(attn_lq2[0].astype(F32) * attn_lk2[0].astype(F32))) + lam_init).reshape(1)
    a = _attn_layer(lam, qq, kk, v, row(attn_subln_g[0]), lam_init)
    out = _attn_out_ffn_layer(h, a, attn_w_o[0].astype(BF16), row(norm_ffn_g[1]), fw1, fw3, fw2,
                              row(final_norm_g), 1)
    return out.reshape(1, L, D_MODEL)
```
